```python
import math
import jax, jax.numpy as jnp
from jax import lax
import numpy as np

D_MODEL = 1024
BATCH = 8
SEQ = 4096
DEPTH = 2
DEC_BATCH = 32
DEC_SEQ = 8
PAST_LEN = 16384
PAGE_SIZE = 128

N_GROUPS_MIX = 4
GROUP_W = D_MODEL // N_GROUPS_MIX
POOL_WINDOWS = (2, 4, 8, 16)
POOL_GROUPS = len(POOL_WINDOWS)
POOL_CH = GROUP_W // POOL_GROUPS
POOL_BUF = max(POOL_WINDOWS) - 1
MOBA_HEADS = 4
HEAD_DIM = GROUP_W // MOBA_HEADS
MOBA_BLOCK = 256
MOBA_TOPK = 3
MOBA_QB = 32
ROPE_DIM = HEAD_DIM // 4
ROPE_THETA = 500000.0
ATTN_SCALE = HEAD_DIM ** -0.5
NEG_BIG = -1e30
CONV_W = 31
CONV_BUF = CONV_W - 1
HGRN_HEADS = 4
HGRN_DK = GROUP_W // HGRN_HEADS
HGRN_DV = GROUP_W // HGRN_HEADS
HGRN_CHUNK = 64
D_FF = 2816
N_IN_SPLITS = 10
IN_COLS = N_IN_SPLITS * GROUP_W
NORM_EPS = 1e-6

kernel_name = "hymba_pool_moba_conv_hgrn2_decode_step"


def rmsnorm(x, g):
    xf = x.astype(jnp.float32)
    y = xf * lax.rsqrt(jnp.mean(xf * xf, axis=-1, keepdims=True) + NORM_EPS)
    return (y * g).astype(x.dtype)


def layernorm(x, g, b):
    xf = x.astype(jnp.float32)
    mu = jnp.mean(xf, axis=-1, keepdims=True)
    var = jnp.mean(jnp.square(xf - mu), axis=-1, keepdims=True)
    return ((xf - mu) * lax.rsqrt(var + NORM_EPS) * g + b).astype(x.dtype)


def swiglu(h, wg, wu, wd):
    return (jax.nn.silu(h @ wg) * (h @ wu)) @ wd


def partial_rope(x, pos):
    half = ROPE_DIM // 2
    inv = ROPE_THETA ** (-jnp.arange(half, dtype=jnp.float32) * 2.0 / ROPE_DIM)
    ang = pos.astype(jnp.float32)[:, None] * inv
    cos = jnp.cos(ang)[None, :, None, :]
    sin = jnp.sin(ang)[None, :, None, :]
    xr = x[..., :ROPE_DIM].astype(jnp.float32)
    x1, x2 = xr[..., :half], xr[..., half:]
    rot = jnp.concatenate([x1 * cos - x2 * sin, x2 * cos + x1 * sin], axis=-1).astype(x.dtype)
    return jnp.concatenate([rot, x[..., ROPE_DIM:]], axis=-1)


def pool_mix(u, prev, pos, w_grp, scale):
    B, T, _ = u.shape
    xp = jnp.concatenate([prev.astype(u.dtype), u], axis=1)
    cs = jnp.cumsum(xp.astype(jnp.float32), axis=1)
    cs = jnp.concatenate([jnp.zeros_like(cs[:, :1]), cs], axis=1)
    end = cs[:, POOL_BUF + 1:]
    outs = []
    for gi, w in enumerate(POOL_WINDOWS):
        sl = slice(gi * POOL_CH, (gi + 1) * POOL_CH)
        s = end[..., sl] - cs[:, POOL_BUF + 1 - w:POOL_BUF + 1 - w + T, sl]
        cnt = jnp.minimum(w, pos + 1).astype(jnp.float32)[None, :, None]
        outs.append(s / cnt)
    pooled = (jnp.concatenate(outs, axis=-1) - u.astype(jnp.float32)).astype(u.dtype)
    pooled = pooled.reshape(B, T, POOL_GROUPS, POOL_CH)
    y = jnp.einsum('btgc,gcd->btgd', pooled, w_grp).reshape(B, T, GROUP_W) * scale
    return y, xp[:, -POOL_BUF:]


def moba_attend(q, k_all, v_all, q_pos0):
    B, Tq, H, dh = q.shape
    Tk = k_all.shape[1]
    nblk = -(-Tk // MOBA_BLOCK)
    pad = nblk * MOBA_BLOCK - Tk
    kb = jnp.pad(k_all, ((0, 0), (0, pad), (0, 0), (0, 0))).reshape(B, nblk, MOBA_BLOCK, H, dh)
    vb = jnp.pad(v_all, ((0, 0), (0, pad), (0, 0), (0, 0))).reshape(B, nblk, MOBA_BLOCK, H, dh)
    kmean = jnp.mean(kb, axis=2, dtype=jnp.float32)
    qpos = q_pos0 + jnp.arange(Tq, dtype=jnp.int32)
    own = qpos // MOBA_BLOCK
    gate = jnp.einsum('bqhd,bnhd->bhqn', q.astype(jnp.float32), kmean)
    past_ok = jnp.arange(nblk)[None, :] < own[:, None]
    gate = jnp.where(past_ok[None, None], gate, NEG_BIG)
    kk = min(MOBA_TOPK, nblk)
    _, sel = lax.top_k(gate, kk)
    blocks = jnp.concatenate(
        [sel.astype(jnp.int32), jnp.broadcast_to(own[None, None, :, None], (B, H, Tq, 1))], axis=-1)
    qb = MOBA_QB if Tq % MOBA_QB == 0 else Tq
    nq = Tq // qb
    q_c = q.reshape(B, nq, qb, H, dh).transpose(1, 0, 3, 2, 4)
    blk_c = blocks.reshape(B, H, nq, qb, kk + 1).transpose(2, 0, 1, 3, 4)
    pos_c = qpos.reshape(nq, qb)
    own_c = own.reshape(nq, qb)
    b_idx = jnp.arange(B)[:, None, None, None]
    h_idx = jnp.arange(H)[None, :, None, None]
    is_own = jnp.arange(kk + 1) == kk
    offs = jnp.arange(MOBA_BLOCK, dtype=jnp.int32)

    def attend(args):
        qc, bc, pc, oc = args
        gk = kb[b_idx, bc, :, h_idx, :]
        gv = vb[b_idx, bc, :, h_idx, :]
        s = jnp.einsum('bhqd,bhqjkd->bhqjk', qc, gk).astype(jnp.float32) * ATTN_SCALE
        kpos = bc[..., None] * MOBA_BLOCK + offs
        blk_ok = is_own[None, None, None, :] | (bc < oc[None, None, :, None])
        ok = (kpos <= pc[None, None, :, None, None]) & blk_ok[..., None]
        s = jnp.where(ok, s, NEG_BIG)
        p = jax.nn.softmax(s.reshape(B, H, qb, -1), axis=-1).reshape(s.shape)
        p = jnp.where(ok, p, 0.0).astype(gv.dtype)
        return jnp.einsum('bhqjk,bhqjkd->bhqd', p, gv)

    o = lax.map(attend, (q_c, blk_c, pos_c, own_c))
    return o.transpose(1, 0, 3, 2, 4).reshape(B, Tq, H * dh)


def conv_mix(a, b, prev, w_dw, b_dw, ln_g, ln_b, w_pw):
    u = a * jax.nn.sigmoid(b)
    xp = jnp.concatenate([prev.astype(u.dtype), u], axis=1)
    y = lax.conv_general_dilated(xp, w_dw[:, None, :].astype(xp.dtype), window_strides=(1,),
                                 padding='VALID', dimension_numbers=('NWC', 'WIO', 'NWC'),
                                 feature_group_count=GROUP_W) + b_dw
    y = jax.nn.silu(layernorm(y, ln_g, ln_b))
    return y @ w_pw, xp[:, -CONV_BUF:]


def hgrn2_mix(hq, hf, hi, hg, S0, lb, norm_g):
    B, T, _ = hq.shape
    f32 = jnp.float32
    C = HGRN_CHUNK if T % HGRN_CHUNK == 0 else T
    n = T // C
    fgate = lb + (1.0 - lb) * jax.nn.sigmoid(hf.astype(f32))
    logf = jnp.log(fgate)
    kin = 1.0 - fgate

    def chunks(a, d):
        return a.astype(f32).reshape(B, n, C, HGRN_HEADS, d).transpose(1, 0, 3, 2, 4)

    tri = jnp.tril(jnp.ones((C, C), dtype=bool))[:, :, None]

    def step(S, inp):
        qc, kc, vc, lc = inp
        bc = jnp.cumsum(lc, axis=2)
        o_inter = jnp.einsum('bhtk,bhkv->bhtv', qc * jnp.exp(bc), S)
        diff = bc[:, :, :, None, :] - bc[:, :, None, :, :]
        dec = jnp.where(tri, jnp.exp(jnp.minimum(diff, 0.0)), 0.0)
        att = jnp.einsum('bhtk,bhtsk,bhsk->bhts', qc, dec, kc)
        o = o_inter + jnp.einsum('bhts,bhsv->bhtv', att, vc)
        blast = bc[:, :, -1:, :]
        S = jnp.exp(blast[:, :, 0, :])[..., None] * S + jnp.einsum(
            'bhsk,bhsv->bhkv', kc * jnp.exp(blast - bc), vc)
        return S, o

    S_fin, o = lax.scan(step, S0.astype(f32),
                        (chunks(hq, HGRN_DK), chunks(kin, HGRN_DK), chunks(hi, HGRN_DV), chunks(logf, HGRN_DK)))
    o = o.transpose(1, 0, 3, 2, 4).reshape(B, T, HGRN_HEADS, HGRN_DV)
    o = rmsnorm(o, norm_g) * jax.nn.silu(hg.astype(f32).reshape(B, T, HGRN_HEADS, HGRN_DV))
    return o.reshape(B, T, GROUP_W).astype(hq.dtype), S_fin.astype(S0.dtype)


def run_trunk(x, q_pos0, pool_st, conv_st, hgrn_st, cache_k, cache_v, page_table,
              ln_ffn1, ffn1_w_gate, ffn1_w_up, ffn1_w_down, ln_mix, w_in, w_out,
              pool_w, pool_scale, q_norm, k_norm, conv_w, conv_b, conv_ln_g, conv_ln_b, conv_pw,
              hgrn_lower_bounds, hgrn_norm, ln_ffn2, ffn2_w_gate, ffn2_w_up, ffn2_w_down):
    B, T, _ = x.shape
    pos = q_pos0 + jnp.arange(T, dtype=jnp.int32)
    lbs = jax.nn.softmax(hgrn_lower_bounds.astype(jnp.float32), axis=0)
    lbs = jnp.cumsum(lbs, axis=0) - lbs[0]
    ks, vs, pools, convs, hgrns = [], [], [], [], []
    for l in range(DEPTH):
        h = rmsnorm(x, ln_ffn1[l])
        x = x + 0.5 * swiglu(h, ffn1_w_gate[l], ffn1_w_up[l], ffn1_w_down[l])
        h = rmsnorm(x, ln_mix[l])
        u_pool, q, k, v, ca, cb, hq, hf, hi, hg = jnp.split(h @ w_in[l], N_IN_SPLITS, axis=-1)
        y_a, pool_new = pool_mix(u_pool, pool_st[l], pos, pool_w[l], pool_scale[l])
        q = partial_rope(rmsnorm(q.reshape(B, T, MOBA_HEADS, HEAD_DIM), q_norm[l]), pos)
        k = partial_rope(rmsnorm(k.reshape(B, T, MOBA_HEADS, HEAD_DIM), k_norm[l]), pos)
        v = v.reshape(B, T, MOBA_HEADS, HEAD_DIM)
        if cache_k is None:
            k_all, v_all = k, v
        else:
            past_k = cache_k[l, page_table].reshape(B, -1, MOBA_HEADS, HEAD_DIM)
            past_v = cache_v[l, page_table].reshape(B, -1, MOBA_HEADS, HEAD_DIM)
            k_all = jnp.concatenate([past_k.astype(k.dtype), k], axis=1)
            v_all = jnp.concatenate([past_v.astype(v.dtype), v], axis=1)
        y_b = moba_attend(q, k_all, v_all, q_pos0)
        y_c, conv_new = conv_mix(ca, cb, conv_st[l], conv_w[l], conv_b[l], conv_ln_g[l], conv_ln_b[l], conv_pw[l])
        y_d, S_new = hgrn2_mix(hq, hf, hi, hg, hgrn_st[l], lbs[l], hgrn_norm[l])
        x = x + jnp.concatenate([y_a, y_b, y_c, y_d], axis=-1) @ w_out[l]
        h = rmsnorm(x, ln_ffn2[l])
        x = x + 0.5 * swiglu(h, ffn2_w_gate[l], ffn2_w_up[l], ffn2_w_down[l])
        ks.append(k)
        vs.append(v)
        pools.append(pool_new)
        convs.append(conv_new)
        hgrns.append(S_new)
    return x, jnp.stack(ks), jnp.stack(vs), jnp.stack(pools), jnp.stack(convs), jnp.stack(hgrns)


def setup_inputs(seed: int = 0) -> dict:
    key = jax.random.key(seed)
    ks = jax.random.split(key, 40)
    f32 = jnp.float32
    n_pages = PAST_LEN // PAGE_SIZE
    n_used = DEC_BATCH * n_pages
    n_pool = n_used + n_used // 4

    def nrm(k, shape, scale):
        return jax.random.normal(k, shape, f32) * scale

    def gain(k, shape):
        return 1.0 + 0.1 * jax.random.normal(k, shape, f32)

    page_table = jax.random.permutation(ks[4], n_pool)[:n_used].reshape(DEC_BATCH, n_pages).astype(jnp.int32)
    return {
        "x_prompt": nrm(ks[0], (BATCH, SEQ, D_MODEL), 1.0),
        "x_sample": nrm(ks[1], (DEC_BATCH, DEC_SEQ, D_MODEL), 1.0),
        "cache_k": nrm(ks[2], (DEPTH, n_pool, PAGE_SIZE, MOBA_HEADS, HEAD_DIM), 1.0),
        "cache_v": nrm(ks[3], (DEPTH, n_pool, PAGE_SIZE, MOBA_HEADS, HEAD_DIM), 1.0),
        "page_table": page_table,
        "state_pool": nrm(ks[5], (DEPTH, DEC_BATCH, POOL_BUF, GROUP_W), 1.0),
        "state_conv": nrm(ks[6], (DEPTH, DEC_BATCH, CONV_BUF, GROUP_W), 0.5),
        "state_hgrn": nrm(ks[7], (DEPTH, DEC_BATCH, HGRN_HEADS, HGRN_DK, HGRN_DV), 0.3),
        "ln_ffn1": gain(ks[8], (DEPTH, D_MODEL)),
        "ffn1_w_gate": nrm(ks[9], (DEPTH, D_MODEL, D_FF), D_MODEL ** -0.5),
        "ffn1_w_up": nrm(ks[10], (DEPTH, D_MODEL, D_FF), D_MODEL ** -0.5),
        "ffn1_w_down": nrm(ks[11], (DEPTH, D_FF, D_MODEL), D_FF ** -0.5),
        "ln_mix": gain(ks[12], (DEPTH, D_MODEL)),
        "w_in": nrm(ks[13], (DEPTH, D_MODEL, IN_COLS), D_MODEL ** -0.5),
        "w_out": nrm(ks[14], (DEPTH, D_MODEL, D_MODEL), D_MODEL ** -0.5),
        "pool_w": nrm(ks[15], (DEPTH, POOL_GROUPS, POOL_CH, POOL_CH), POOL_CH ** -0.5),
        "pool_scale": gain(ks[16], (DEPTH, GROUP_W)),
        "q_norm": gain(ks[17], (DEPTH, HEAD_DIM)),
        "k_norm": gain(ks[18], (DEPTH, HEAD_DIM)),
        "conv_w": nrm(ks[19], (DEPTH, CONV_W, GROUP_W), CONV_W ** -0.5),
        "conv_b": nrm(ks[20], (DEPTH, GROUP_W), 0.02),
        "conv_ln_g": gain(ks[21], (DEPTH, GROUP_W)),
        "conv_ln_b": nrm(ks[22], (DEPTH, GROUP_W), 0.02),
        "conv_pw": nrm(ks[23], (DEPTH, GROUP_W, GROUP_W), GROUP_W ** -0.5),
        "hgrn_lower_bounds": nrm(ks[24], (DEPTH, GROUP_W), 0.5),
        "hgrn_norm": gain(ks[25], (DEPTH, HGRN_DV)),
        "ln_ffn2": gain(ks[26], (DEPTH, D_MODEL)),
        "ffn2_w_gate": nrm(ks[27], (DEPTH, D_MODEL, D_FF), D_MODEL ** -0.5),
        "ffn2_w_up": nrm(ks[28], (DEPTH, D_MODEL, D_FF), D_MODEL ** -0.5),
        "ffn2_w_down": nrm(ks[29], (DEPTH, D_FF, D_MODEL), D_FF ** -0.5),
    }


def reference(x_prompt, x_sample, cache_k, cache_v, page_table, state_pool, state_conv, state_hgrn,
              ln_ffn1, ffn1_w_gate, ffn1_w_up, ffn1_w_down, ln_mix, w_in, w_out,
              pool_w, pool_scale, q_norm, k_norm, conv_w, conv_b, conv_ln_g, conv_ln_b, conv_pw,
              hgrn_lower_bounds, hgrn_norm, ln_ffn2, ffn2_w_gate, ffn2_w_up, ffn2_w_down):
    weights = (ln_ffn1, ffn1_w_gate, ffn1_w_up, ffn1_w_down, ln_mix, w_in, w_out,
               pool_w, pool_scale, q_norm, k_norm, conv_w, conv_b, conv_ln_g, conv_ln_b, conv_pw,
               hgrn_lower_bounds, hgrn_norm, ln_ffn2, ffn2_w_gate, ffn2_w_up, ffn2_w_down)
    bp = x_prompt.shape[0]
    dt = x_prompt.dtype
    zero_pool = jnp.zeros((DEPTH, bp, POOL_BUF, GROUP_W), dt)
    zero_conv = jnp.zeros((DEPTH, bp, CONV_BUF, GROUP_W), dt)
    zero_hgrn = jnp.zeros((DEPTH, bp, HGRN_HEADS, HGRN_DK, HGRN_DV), dt)
    y_prompt, k_prompt, v_prompt, pool_prompt, conv_prompt, hgrn_prompt = run_trunk(
        x_prompt, 0, zero_pool, zero_conv, zero_hgrn, None, None, None, *weights)
    past_len = page_table.shape[1] * cache_k.shape[2]
    y_sample, k_sample, v_sample, pool_sample, conv_sample, hgrn_sample = run_trunk(
        x_sample, past_len, state_pool, state_conv, state_hgrn, cache_k, cache_v, page_table, *weights)
    return (y_prompt, y_sample, k_prompt, v_prompt, k_sample, v_sample,
            pool_prompt, pool_sample, conv_prompt, conv_sample, hgrn_prompt, hgrn_sample)
```

```python
import functools

import jax
import jax.numpy as jnp
from jax import lax
from jax.experimental import pallas as pl
from jax.experimental.pallas import tpu as pltpu

F32 = jnp.float32
BF16 = jnp.bfloat16
HI = lax.Precision.HIGHEST

NORM_EPS = 1e-6
GROUP_W = 256
N_HEADS = 4
HEAD_DIM = GROUP_W // N_HEADS
N_IN_SPLITS = 10
POOL_WINDOWS = (2, 4, 8, 16)
POOL_PAD = 16
CONV_W = 31
CONV_PAD = 32
MOBA_BLOCK = 256
MOBA_TOPK = 3
ROPE_DIM = HEAD_DIM // 4
ROPE_THETA = 500000.0
ATTN_SCALE = HEAD_DIM ** -0.5
NEG_BIG = -1e30
HGRN_CHUNK = 64
HGRN_SUB = 16
PAGES_PER_CHUNK = 16
VMEM_LIMIT = 56 * 1024 * 1024


def _params(sem=None):
    return pltpu.CompilerParams(dimension_semantics=sem, vmem_limit_bytes=VMEM_LIMIT)


def _dot(a, b):
    return jnp.dot(a, b, preferred_element_type=F32)


def _dot_nt(a, b, precision=None):
    return lax.dot_general(a, b, (((1,), (1,)), ((), ())), precision=precision,
                           preferred_element_type=F32)


def _lane_head(shape):
    return lax.broadcasted_iota(jnp.int32, shape, len(shape) - 1) // HEAD_DIM


def _head_ones():
    r = lax.broadcasted_iota(jnp.int32, (GROUP_W, GROUP_W), 0) // HEAD_DIM
    c = lax.broadcasted_iota(jnp.int32, (GROUP_W, GROUP_W), 1) // HEAD_DIM
    return r == c


def _sigmoid(x):
    return 1.0 / (1.0 + jnp.exp(-x))


def _ffn_kernel(*refs, n_ff, merge):
    if merge:
        (x_ref, ya_ref, yb_ref, yc_ref, yd_ref, wo_ref, g_ref, wg_ref, wu_ref, wd_ref,
         o_ref, x1_ref, h_ref, acc_ref) = refs
    else:
        x_ref, g_ref, wg_ref, wu_ref, wd_ref, o_ref, h_ref, acc_ref = refs
    j = pl.program_id(1)

    @pl.when(j == 0)
    def _():
        x = x_ref[...]
        if merge:
            for idx, y_ref in enumerate((ya_ref, yb_ref, yc_ref, yd_ref)):
                x = x + _dot(y_ref[...].astype(BF16), wo_ref[idx * GROUP_W:(idx + 1) * GROUP_W, :])
            x1_ref[...] = x
        ms = jnp.mean(x * x, axis=-1, keepdims=True)
        h_ref[...] = (x * lax.rsqrt(ms + NORM_EPS) * g_ref[...]).astype(BF16)
        acc_ref[...] = jnp.zeros_like(acc_ref)

    h = h_ref[...]
    g = _dot(h, wg_ref[...])
    u = _dot(h, wu_ref[...])
    a = (g * _sigmoid(g) * u).astype(BF16)
    acc_ref[...] += _dot(a, wd_ref[...])

    @pl.when(j == n_ff - 1)
    def _():
        base = x1_ref[...] if merge else x_ref[...]
        o_ref[...] = base + 0.5 * acc_ref[...]


def _ffn(x, gain, wg, wu, wd, mix=None, w_out=None):
    n, d = x.shape
    d_ff = wg.shape[1]
    tm = 1024 if n % 1024 == 0 else n
    tf = 256
    n_ff = d_ff // tf
    merge = mix is not None
    tok = lambda i, j: (i, 0)
    in_specs = [pl.BlockSpec((tm, d), tok)]
    args = [x]
    scratch = []
    if merge:
        in_specs += [pl.BlockSpec((tm, GROUP_W), tok)] * 4 + [pl.BlockSpec((d, d), lambda i, j: (0, 0))]
        args += list(mix) + [w_out]
        scratch.append(pltpu.VMEM((tm, d), F32))
    in_specs += [pl.BlockSpec((1, d), lambda i, j: (0, 0)),
                 pl.BlockSpec((d, tf), lambda i, j: (0, j)),
                 pl.BlockSpec((d, tf), lambda i, j: (0, j)),
                 pl.BlockSpec((tf, d), lambda i, j: (j, 0))]
    args += [gain.reshape(1, d), wg, wu, wd]
    scratch += [pltpu.VMEM((tm, d), BF16), pltpu.VMEM((tm, d), F32)]
    return pl.pallas_call(
        functools.partial(_ffn_kernel, n_ff=n_ff, merge=merge),
        grid=(n // tm, n_ff),
        in_specs=in_specs,
        out_specs=pl.BlockSpec((tm, d), tok),
        out_shape=jax.ShapeDtypeStruct((n, d), F32),
        scratch_shapes=scratch,
        compiler_params=_params(("parallel", "arbitrary")),
        name="ffn_merge" if merge else "ffn",
    )(*args)


def _inproj_kernel(*refs, with_kmean):
    (x_ref, g_ref, w_ref, qg_ref, kg_ref, cos_ref, sin_ref,
     up_ref, q_ref, k_ref, v_ref, kb_ref, vb_ref, cu_ref, hg_ref) = refs[:15]
    x = x_ref[...]
    ms = jnp.mean(x * x, axis=-1, keepdims=True)
    h = (x * lax.rsqrt(ms + NORM_EPS) * g_ref[...]).astype(BF16)

    def proj(c):
        return _dot(h, w_ref[:, c * GROUP_W:(c + 1) * GROUP_W])

    ones = _head_ones().astype(F32)
    cos = cos_ref[...]
    sin = sin_ref[...]
    d = lax.broadcasted_iota(jnp.int32, cos.shape, 1) % HEAD_DIM
    half = ROPE_DIM // 2

    def norm_rope(p, gain):
        ms_h = jnp.dot(p * p, ones, precision=HI, preferred_element_type=F32) * (1.0 / HEAD_DIM)
        y = p * lax.rsqrt(ms_h + NORM_EPS) * gain
        partner = jnp.where(d < half, pltpu.roll(y, GROUP_W - half, axis=1), pltpu.roll(y, half, axis=1))
        return y * cos + partner * sin

    up_ref[...] = proj(0)
    q_ref[...] = norm_rope(proj(1), qg_ref[...])
    k = norm_rope(proj(2), kg_ref[...])
    k_ref[...] = k
    kb_ref[...] = k.astype(BF16)
    v = proj(3)
    v_ref[...] = v
    vb_ref[...] = v.astype(BF16)
    cu_ref[...] = proj(4) * _sigmoid(proj(5))
    for c in range(4):
        hg_ref[:, c * GROUP_W:(c + 1) * GROUP_W] = proj(6 + c)
    if with_kmean:
        km_ref = refs[15]
        for n in range(km_ref.shape[0]):
            km_ref[n] = jnp.mean(k[n * MOBA_BLOCK:(n + 1) * MOBA_BLOCK, :], axis=0, keepdims=True)


def _rope_tables(t_len, q_pos0):
    half = ROPE_DIM // 2
    inv = ROPE_THETA ** (-jnp.arange(half, dtype=F32) * 2.0 / ROPE_DIM)
    pos = q_pos0 + jnp.arange(t_len, dtype=jnp.int32)
    ang = pos.astype(F32)[:, None] * inv
    c, s = jnp.cos(ang), jnp.sin(ang)
    rest = HEAD_DIM - ROPE_DIM
    cos_h = jnp.concatenate([c, c, jnp.ones((t_len, rest), F32)], axis=1)
    sin_h = jnp.concatenate([-s, s, jnp.zeros((t_len, rest), F32)], axis=1)
    return jnp.tile(cos_h, (1, N_HEADS)), jnp.tile(sin_h, (1, N_HEADS))


def _inproj(x, t_len, q_pos0, gain, w_in, q_gain, k_gain, with_kmean):
    n, d = x.shape
    if t_len % 512 == 0:
        tm = 512
    else:
        tm = t_len * max(r for r in range(1, n // t_len + 1) if (n // t_len) % r == 0 and t_len * r <= 512)
    cos_t, sin_t = _rope_tables(t_len, q_pos0)
    if tm > t_len:
        cos_t, sin_t = jnp.tile(cos_t, (tm // t_len, 1)), jnp.tile(sin_t, (tm // t_len, 1))
    per_seq = max(t_len // tm, 1)
    tok = lambda i: (i, 0)
    const = lambda i: (0, 0)
    tab = lambda i: (i % per_seq, 0)
    grp = jax.ShapeDtypeStruct((n, GROUP_W), F32)
    grp_b = jax.ShapeDtypeStruct((n, GROUP_W), BF16)
    gspec = pl.BlockSpec((tm, GROUP_W), tok)
    out_shape = [grp, grp, grp, grp, grp_b, grp_b, grp, jax.ShapeDtypeStruct((n, 4 * GROUP_W), F32)]
    out_specs = [gspec] * 7 + [pl.BlockSpec((tm, 4 * GROUP_W), tok)]
    if with_kmean:
        nb = tm // MOBA_BLOCK
        out_shape.append(jax.ShapeDtypeStruct((n // MOBA_BLOCK, 1, GROUP_W), F32))
        out_specs.append(pl.BlockSpec((nb, 1, GROUP_W), lambda i: (i, 0, 0)))
    return pl.pallas_call(
        functools.partial(_inproj_kernel, with_kmean=with_kmean),
        grid=(n // tm,),
        in_specs=[pl.BlockSpec((tm, d), tok), pl.BlockSpec((1, d), const),
                  pl.BlockSpec((d, N_IN_SPLITS * GROUP_W), const),
                  pl.BlockSpec((1, GROUP_W), const), pl.BlockSpec((1, GROUP_W), const),
                  pl.BlockSpec((tm, GROUP_W), tab), pl.BlockSpec((tm, GROUP_W), tab)],
        out_specs=out_specs,
        out_shape=out_shape,
        compiler_params=_params(("parallel",)),
        name="inproj",
    )(x, gain.reshape(1, d), w_in, jnp.tile(q_gain, N_HEADS).reshape(1, GROUP_W),
      jnp.tile(k_gain, N_HEADS).reshape(1, GROUP_W), cos_t, sin_t)


def _poolconv_kernel(up_ref, cu_ref, pprev_ref, cprev_ref, pw_ref, ps_ref, cw_ref, cb_ref, lg_ref, lb_ref, cpw_ref,
                     ya_ref, yc_ref, pnew_ref, cnew_ref, pbuf, cbuf, *, tt, q_pos0, rows):
    t = pl.program_id(1)

    @pl.when(t == 0)
    def _():
        pbuf[0:POOL_PAD, :] = pprev_ref[0]
        cbuf[0:CONV_PAD, :] = cprev_ref[0]

    pbuf[POOL_PAD:POOL_PAD + tt, :] = up_ref[0]
    cbuf[CONV_PAD:CONV_PAD + tt, :] = cu_ref[0]

    for r0 in range(0, tt, rows):
        pos1 = (q_pos0 + t * tt + r0 + 1 + lax.broadcasted_iota(jnp.int32, (rows, 128), 0)).astype(F32)
        first = lax.broadcasted_iota(jnp.int32, (rows, 128), 1) < (GROUP_W // 4)
        pooled = []
        for half_i, (w_a, w_b) in enumerate(((POOL_WINDOWS[0], POOL_WINDOWS[1]), (POOL_WINDOWS[2], POOL_WINDOWS[3]))):
            lanes = slice(half_i * 128, (half_i + 1) * 128)
            u = pbuf[POOL_PAD + r0:POOL_PAD + r0 + rows, lanes]
            acc = u
            s_a = None
            for j in range(1, w_b):
                acc = acc + pbuf[POOL_PAD + r0 - j:POOL_PAD + r0 - j + rows, lanes]
                if j == w_a - 1:
                    s_a = acc
            avg = jnp.where(first, s_a / jnp.minimum(float(w_a), pos1), acc / jnp.minimum(float(w_b), pos1))
            pooled.append((avg - u).astype(BF16))
        pooled = jnp.concatenate(pooled, axis=1)
        ya_ref[0, r0:r0 + rows, :] = _dot(pooled, pw_ref[...]) * ps_ref[...]

        acc = cbuf[CONV_PAD - (CONV_W - 1) + r0:CONV_PAD - (CONV_W - 1) + r0 + rows, :] * cw_ref[0:1, :]
        for j in range(1, CONV_W):
            base = CONV_PAD - (CONV_W - 1) + j + r0
            acc = acc + cbuf[base:base + rows, :] * cw_ref[j:j + 1, :]
        y = acc + cb_ref[...]
        mu = jnp.mean(y, axis=-1, keepdims=True)
        var = jnp.mean(jnp.square(y - mu), axis=-1, keepdims=True)
        y = (y - mu) * lax.rsqrt(var + NORM_EPS) * lg_ref[...] + lb_ref[...]
        y = y * _sigmoid(y)
        yc_ref[0, r0:r0 + rows, :] = _dot(y.astype(BF16), cpw_ref[...])

    ptail = pbuf[tt:tt + POOL_PAD, :]
    ctail = cbuf[tt:tt + CONV_PAD, :]
    pnew_ref[0] = ptail
    cnew_ref[0] = ctail
    pbuf[0:POOL_PAD, :] = ptail
    cbuf[0:CONV_PAD, :] = ctail


def _poolconv(up, cu, pool_prev, conv_prev, q_pos0, pool_w, pool_scale, conv_w, conv_b, ln_g, ln_b, conv_pw):
    b, t_len, _ = up.shape
    tt = 512 if t_len % 512 == 0 else t_len
    rows = 64 if tt % 64 == 0 else tt
    pprev = jnp.pad(pool_prev, ((0, 0), (POOL_PAD - pool_prev.shape[1], 0), (0, 0)))
    cprev = jnp.pad(conv_prev, ((0, 0), (CONV_PAD - conv_prev.shape[1], 0), (0, 0)))
    pw_bd = jax.scipy.linalg.block_diag(*[pool_w[g] for g in range(pool_w.shape[0])]).astype(BF16)
    cw_pad = jnp.pad(conv_w, ((0, CONV_PAD - CONV_W), (0, 0)))
    seq = lambda i, j: (i, j, 0)
    per_b = lambda i, j: (i, 0, 0)
    const = lambda i, j: (0, 0)
    row = pl.BlockSpec((1, GROUP_W), const)
    ya, yc, pnew, cnew = pl.pallas_call(
        functools.partial(_poolconv_kernel, tt=tt, q_pos0=q_pos0, rows=rows),
        grid=(b, t_len // tt),
        in_specs=[pl.BlockSpec((1, tt, GROUP_W), seq), pl.BlockSpec((1, tt, GROUP_W), seq),
                  pl.BlockSpec((1, POOL_PAD, GROUP_W), per_b), pl.BlockSpec((1, CONV_PAD, GROUP_W), per_b),
                  pl.BlockSpec((GROUP_W, GROUP_W), const), row,
                  pl.BlockSpec((CONV_PAD, GROUP_W), const), row, row, row,
                  pl.BlockSpec((GROUP_W, GROUP_W), const)],
        out_specs=[pl.BlockSpec((1, tt, GROUP_W), seq), pl.BlockSpec((1, tt, GROUP_W), seq),
                   pl.BlockSpec((1, POOL_PAD, GROUP_W), per_b), pl.BlockSpec((1, CONV_PAD, GROUP_W), per_b)],
        out_shape=[jax.ShapeDtypeStruct((b, t_len, GROUP_W), F32), jax.ShapeDtypeStruct((b, t_len, GROUP_W), F32),
                   jax.ShapeDtypeStruct((b, POOL_PAD, GROUP_W), F32), jax.ShapeDtypeStruct((b, CONV_PAD, GROUP_W), F32)],
        scratch_shapes=[pltpu.VMEM((POOL_PAD + tt, GROUP_W), F32), pltpu.VMEM((CONV_PAD + tt, GROUP_W), F32)],
        compiler_params=_params(("parallel", "arbitrary")),
        name="poolconv",
    )(up, cu, pprev, cprev, pw_bd, pool_scale.reshape(1, GROUP_W), cw_pad, conv_b.reshape(1, GROUP_W),
      ln_g.reshape(1, GROUP_W), ln_b.reshape(1, GROUP_W), conv_pw.astype(BF16))
    return ya, yc, pnew[:, POOL_PAD - pool_prev.shape[1]:], cnew[:, CONV_PAD - conv_prev.shape[1]:]


def _stack_heads(q):
    head = _lane_head(q.shape)
    return jnp.concatenate([jnp.where(head == h, q, 0.0) for h in range(N_HEADS)], axis=0)


def _select_blocks(gate, n_past):
    lane = lax.broadcasted_iota(jnp.int32, gate.shape, 1)
    past = lane < n_past
    g = jnp.where(past, gate, NEG_BIG)
    sel = jnp.zeros(gate.shape, jnp.bool_)
    for _ in range(MOBA_TOPK):
        mx = jnp.max(g, axis=1, keepdims=True)
        idx = jnp.min(jnp.where(g == mx, lane, 1 << 20), axis=1, keepdims=True)
        pick = lane == idx
        sel = jnp.logical_or(sel, pick)
        g = jnp.where(pick, -jnp.inf, g)
    return jnp.logical_and(sel, past)


def _moba_prefill_kernel(q_ref, kb_ref, vb_ref, km_ref, o_ref, kmp_ref, sel_ref, m_ref, l_ref, acc_ref, *, nblk):
    i = pl.program_id(1)
    blk = MOBA_BLOCK
    rows = N_HEADS * blk

    @pl.when(i == 0)
    def _():
        kmp_ref[...] = jnp.zeros_like(kmp_ref)
        for n in range(nblk):
            kmp_ref[n:n + 1, :] = km_ref[0, n]

    q4 = _stack_heads(q_ref[0])
    gate = _dot_nt(q4, kmp_ref[...], precision=HI)
    sel = _select_blocks(gate, i).astype(F32)
    for n in range(nblk):
        @pl.when(n < i)
        def _():
            sel_ref[n] = jnp.broadcast_to(sel[:, n:n + 1], (rows, 128))

    q4b = (q4 * ATTN_SCALE).astype(BF16)

    own0 = pl.multiple_of(i * blk, blk)
    s = _dot_nt(q4b, kb_ref[0, pl.ds(own0, blk), :])
    qi = lax.broadcasted_iota(jnp.int32, (rows, blk), 0) % blk
    ki = lax.broadcasted_iota(jnp.int32, (rows, blk), 1)
    s = jnp.where(ki <= qi, s, NEG_BIG)
    m = jnp.max(s, axis=1, keepdims=True)
    p = jnp.exp(s - m)
    m_ref[...] = jnp.broadcast_to(m, (rows, 128))
    l_ref[...] = jnp.broadcast_to(jnp.sum(p, axis=1, keepdims=True), (rows, 128))
    acc_ref[...] = _dot(p.astype(BF16), vb_ref[0, pl.ds(own0, blk), :])

    def body(j, carry):
        j0 = pl.multiple_of(j * blk, blk)
        s = _dot_nt(q4b, kb_ref[0, pl.ds(j0, blk), :])
        ok = sel_ref[j][:, 0:1] > 0.5
        s = jnp.where(ok, s, NEG_BIG)
        m_old = m_ref[...]
        m_new = jnp.maximum(m_old, jnp.max(s, axis=1, keepdims=True))
        alpha = jnp.exp(m_old - m_new)
        p = jnp.exp(s - m_new[:, 0:1])
        l_ref[...] = alpha * l_ref[...] + jnp.sum(p, axis=1, keepdims=True)
        m_ref[...] = m_new
        acc_ref[...] = alpha[:, 0:1] * acc_ref[...] + _dot(p.astype(BF16), vb_ref[0, pl.ds(j0, blk), :])
        return carry

    lax.fori_loop(0, i, body, 0)

    head = _lane_head((blk, GROUP_W))
    out = jnp.zeros((blk, GROUP_W), F32)
    for h in range(N_HEADS):
        rs = slice(h * blk, (h + 1) * blk)
        out = out + jnp.where(head == h, acc_ref[rs, :] / l_ref[rs, 0:1], 0.0)
    o_ref[0] = out


def _moba_prefill(q, kb, vb, kmean):
    b, t_len, _ = q.shape
    nblk = t_len // MOBA_BLOCK
    rows = N_HEADS * MOBA_BLOCK
    return pl.pallas_call(
        functools.partial(_moba_prefill_kernel, nblk=nblk),
        grid=(b, nblk),
        in_specs=[pl.BlockSpec((1, MOBA_BLOCK, GROUP_W), lambda i, j: (i, j, 0)),
                  pl.BlockSpec((1, t_len, GROUP_W), lambda i, j: (i, 0, 0)),
                  pl.BlockSpec((1, t_len, GROUP_W), lambda i, j: (i, 0, 0)),
                  pl.BlockSpec((1, nblk, 1, GROUP_W), lambda i, j: (i, 0, 0, 0))],
        out_specs=pl.BlockSpec((1, MOBA_BLOCK, GROUP_W), lambda i, j: (i, j, 0)),
        out_shape=jax.ShapeDtypeStruct((b, t_len, GROUP_W), F32),
        scratch_shapes=[pltpu.VMEM((128, GROUP_W), F32), pltpu.VMEM((nblk, rows, 128), F32),
                        pltpu.VMEM((rows, 128), F32), pltpu.VMEM((rows, 128), F32),
                        pltpu.VMEM((rows, GROUP_W), F32)],
        compiler_params=_params(("parallel", "arbitrary")),
        name="moba_prefill",
    )(q, kb, vb, kmean)


def _moba_decode_kernel(pt_ref, q_ref, kn_ref, vn_ref, ck_ref, cv_ref, o_ref,
                        buf, sem, s_ref, p_ref, kmp_ref, own_ref, *, layer, n_pages, tq):
    b = pl.program_id(0)
    ch = PAGES_PER_CHUNK
    n_chunks = n_pages // ch
    page = buf.shape[2]
    ckeys = ch * page
    n_past = n_pages * page // MOBA_BLOCK
    blk_per_chunk = ckeys // MOBA_BLOCK
    rows = N_HEADS * tq

    def page_copy(cache_ref, c, pg, slot):
        return pltpu.make_async_copy(cache_ref.at[layer, pt_ref[b, c * ch + pg]], buf.at[slot, pg], sem.at[slot])

    def start_chunk(cache_ref, c, slot):
        for pg in range(ch):
            page_copy(cache_ref, c, pg, slot).start()

    def wait_chunk(cache_ref, c, slot):
        for pg in range(ch):
            page_copy(cache_ref, c, pg, slot).wait()

    q4 = _stack_heads(q_ref[0])
    q4b = (q4 * ATTN_SCALE).astype(BF16)
    kmp_ref[...] = jnp.zeros_like(kmp_ref)

    start_chunk(ck_ref, 0, 0)
    for c in range(n_chunks):
        slot = c % 2
        if c + 1 < n_chunks:
            start_chunk(ck_ref, c + 1, 1 - slot)
        else:
            start_chunk(cv_ref, 0, 1 - slot)
        wait_chunk(ck_ref, c, slot)
        kc = buf[slot].reshape(ckeys, GROUP_W)
        s_ref[:, c * ckeys:(c + 1) * ckeys] = _dot_nt(q4b, kc.astype(BF16))
        for n in range(blk_per_chunk):
            row = c * blk_per_chunk + n
            kmp_ref[row:row + 1, :] = jnp.mean(kc[n * MOBA_BLOCK:(n + 1) * MOBA_BLOCK, :], axis=0, keepdims=True)

    gate = _dot_nt(q4, kmp_ref[...], precision=HI)
    sel = _select_blocks(gate, n_past).astype(F32)

    own_ref[...] = jnp.zeros_like(own_ref)
    own_ref[0, 0:tq, :] = kn_ref[0]
    own_ref[1, 0:tq, :] = vn_ref[0]
    s_own = _dot_nt(q4b, own_ref[0].astype(BF16))
    qi = lax.broadcasted_iota(jnp.int32, (rows, 128), 0) % tq
    ki = lax.broadcasted_iota(jnp.int32, (rows, 128), 1)
    s_own = jnp.where(ki <= qi, s_own, NEG_BIG)

    m = jnp.max(s_own, axis=1, keepdims=True)
    for n in range(n_past):
        s_blk = s_ref[:, n * MOBA_BLOCK:(n + 1) * MOBA_BLOCK]
        s_blk = jnp.where(sel[:, n:n + 1] > 0.5, s_blk, NEG_BIG)
        s_ref[:, n * MOBA_BLOCK:(n + 1) * MOBA_BLOCK] = s_blk
        m = jnp.maximum(m, jnp.max(s_blk, axis=1, keepdims=True))
    p_own = jnp.exp(s_own - m)
    l = jnp.sum(p_own, axis=1, keepdims=True)
    for n in range(n_past):
        p = jnp.exp(s_ref[:, n * MOBA_BLOCK:(n + 1) * MOBA_BLOCK] - m)
        l = l + jnp.sum(p, axis=1, keepdims=True)
        p_ref[:, n * MOBA_BLOCK:(n + 1) * MOBA_BLOCK] = p.astype(BF16)

    acc = _dot(p_own.astype(BF16), own_ref[1].astype(BF16))
    for c in range(n_chunks):
        slot = (n_chunks + c) % 2
        if c + 1 < n_chunks:
            start_chunk(cv_ref, c + 1, 1 - slot)
        wait_chunk(cv_ref, c, slot)
        vc = buf[slot].reshape(ckeys, GROUP_W).astype(BF16)
        acc = acc + _dot(p_ref[:, c * ckeys:(c + 1) * ckeys], vc)

    acc = acc / l
    head = _lane_head((tq, GROUP_W))
    out = jnp.zeros((tq, GROUP_W), F32)
    for h in range(N_HEADS):
        out = out + jnp.where(head == h, acc[h * tq:(h + 1) * tq, :], 0.0)
    o_ref[0] = out


def _moba_decode(q, k_new, v_new, cache_k, cache_v, page_table, layer):
    b, tq, _ = q.shape
    n_pages = page_table.shape[1]
    page = cache_k.shape[2]
    n_keys = n_pages * page
    assert n_keys % MOBA_BLOCK == 0 and n_pages % PAGES_PER_CHUNK == 0 and (n_pages // PAGES_PER_CHUNK) % 2 == 0
    assert (PAGES_PER_CHUNK * page) % MOBA_BLOCK == 0 and tq <= 128 and n_keys // MOBA_BLOCK <= 128
    rows = N_HEADS * tq
    per_b = lambda i, pt: (i, 0, 0)
    grid_spec = pltpu.PrefetchScalarGridSpec(
        num_scalar_prefetch=1,
        grid=(b,),
        in_specs=[pl.BlockSpec((1, tq, GROUP_W), per_b), pl.BlockSpec((1, tq, GROUP_W), per_b),
                  pl.BlockSpec((1, tq, GROUP_W), per_b),
                  pl.BlockSpec(memory_space=pl.ANY), pl.BlockSpec(memory_space=pl.ANY)],
        out_specs=pl.BlockSpec((1, tq, GROUP_W), per_b),
        scratch_shapes=[pltpu.VMEM((2, PAGES_PER_CHUNK, page, GROUP_W), F32),
                        pltpu.SemaphoreType.DMA((2,)),
                        pltpu.VMEM((rows, n_keys), F32), pltpu.VMEM((rows, n_keys), BF16),
                        pltpu.VMEM((128, GROUP_W), F32), pltpu.VMEM((2, 128, GROUP_W), F32)],
    )
    return pl.pallas_call(
        functools.partial(_moba_decode_kernel, layer=layer, n_pages=n_pages, tq=tq),
        grid_spec=grid_spec,
        out_shape=jax.ShapeDtypeStruct((b, tq, GROUP_W), F32),
        compiler_params=_params(("arbitrary",)),
        name="moba_decode",
    )(page_table, q, k_new, v_new, cache_k, cache_v)


def _hgrn_kernel(x_ref, st0_ref, lb_ref, ng_ref, y_ref, stout_ref, st_ref, *, tt, chunk, sub):
    t = pl.program_id(1)
    n_sub = chunk // sub

    @pl.when(t == 0)
    def _():
        st_ref[...] = st0_ref[0]

    lb = lb_ref[...]
    head = _lane_head((1, GROUP_W))
    hmask = [head == h for h in range(N_HEADS)]
    bd = _head_ones()
    ones_b = bd.astype(BF16)
    ones_f = bd.astype(F32)
    tri = (lax.broadcasted_iota(jnp.int32, (chunk, chunk), 1)
           <= lax.broadcasted_iota(jnp.int32, (chunk, chunk), 0)).astype(F32)
    t_ge = [lax.broadcasted_iota(jnp.int32, (sub, GROUP_W), 0) >= s for s in range(sub)]

    def one_chunk(r0):
        hq = x_ref[0, pl.ds(r0, chunk), 0 * GROUP_W:1 * GROUP_W]
        hf = x_ref[0, pl.ds(r0, chunk), 1 * GROUP_W:2 * GROUP_W]
        hi = x_ref[0, pl.ds(r0, chunk), 2 * GROUP_W:3 * GROUP_W]
        hg = x_ref[0, pl.ds(r0, chunk), 3 * GROUP_W:4 * GROUP_W]
        f = lb + (1.0 - lb) * _sigmoid(hf)
        logf = jnp.log(f)
        kin = 1.0 - f
        bc = jnp.dot(tri, logf, precision=HI, preferred_element_type=F32)
        st = st_ref[...]

        o = _dot_nt((hq * jnp.exp(bc)).astype(BF16), st.astype(BF16))

        if n_sub > 1:
            bq = jnp.concatenate(
                [jnp.zeros((sub, GROUP_W), F32)]
                + [jnp.broadcast_to(bc[i * sub - 1:i * sub, :], (sub, GROUP_W)) for i in range(1, n_sub)], axis=0)
            q_off = (hq * jnp.exp(bc - bq)).astype(BF16)
            k_rows, v_rows = [], []
            for i in range(1, n_sub):
                kt = kin[0:i * sub, :] * jnp.exp(bc[i * sub - 1:i * sub, :] - bc[0:i * sub, :])
                for h in range(N_HEADS):
                    k_rows.append(jnp.where(hmask[h], kt, 0.0))
                    v_rows.append(jnp.where(hmask[h], hi[0:i * sub, :], 0.0))
            k_st = jnp.concatenate(k_rows, axis=0).astype(BF16)
            v_st = jnp.concatenate(v_rows, axis=0).astype(BF16)
            att = _dot_nt(q_off, k_st)
            ncol = att.shape[1]
            col = lax.broadcasted_iota(jnp.int32, (chunk, ncol), 1)
            row_sub = lax.broadcasted_iota(jnp.int32, (chunk, ncol), 0) // sub
            keep = jnp.zeros((chunk, ncol), jnp.bool_)
            c0 = 0
            for i in range(1, n_sub):
                width = N_HEADS * i * sub
                keep = jnp.logical_or(keep, jnp.logical_and(row_sub == i,
                                                            jnp.logical_and(col >= c0, col < c0 + width)))
                c0 += width
            o = o + _dot(jnp.where(keep, att, 0.0).astype(BF16), v_st)

        d_rows = []
        for i in range(n_sub):
            rs = slice(i * sub, (i + 1) * sub)
            q_i, bc_i, k_i = hq[rs], bc[rs], kin[rs]
            for s in range(sub):
                w = q_i * k_i[s:s + 1, :] * jnp.exp(jnp.minimum(bc_i - bc_i[s:s + 1, :], 0.0))
                d_rows.append(jnp.where(t_ge[s], w, 0.0).astype(BF16))
        a = _dot(jnp.concatenate(d_rows, axis=0), ones_b)
        o_diag = []
        for i in range(n_sub):
            v_i = hi[i * sub:(i + 1) * sub]
            acc = jnp.zeros((sub, GROUP_W), F32)
            for s in range(sub):
                base = (i * sub + s) * sub
                acc = acc + a[base:base + sub, :] * v_i[s:s + 1, :]
            o_diag.append(acc)
        o = o + (jnp.concatenate(o_diag, axis=0) if n_sub > 1 else o_diag[0])

        b_last = bc[chunk - 1:chunk, :]
        k_state = (kin * jnp.exp(b_last - bc)).astype(BF16)
        upd = _dot(hi.T.astype(BF16), k_state)
        st_ref[...] = jnp.where(bd, st * jnp.exp(b_last) + upd, 0.0)

        ms = jnp.dot(o * o, ones_f, precision=HI, preferred_element_type=F32) * (1.0 / HEAD_DIM)
        y = o * lax.rsqrt(ms + NORM_EPS) * ng_ref[...]
        y_ref[0, pl.ds(r0, chunk), :] = y * (hg * _sigmoid(hg))

    if tt == chunk:
        one_chunk(0)
    else:
        def body(ci, carry):
            one_chunk(pl.multiple_of(ci * chunk, chunk))
            return carry
        lax.fori_loop(0, tt // chunk, body, 0)

    @pl.when(t == pl.num_programs(1) - 1)
    def _():
        stout_ref[0] = st_ref[...]


def _hgrn(x4, state, lb, norm_g):
    b, t_len, _ = x4.shape
    chunk = HGRN_CHUNK if t_len % HGRN_CHUNK == 0 else t_len
    sub = HGRN_SUB if chunk % HGRN_SUB == 0 else chunk
    tt = 512 if t_len % 512 == 0 else (chunk if t_len == chunk else t_len)
    eye = jnp.eye(N_HEADS, dtype=state.dtype)
    st0 = jnp.einsum('bhkv,hg->bhvgk', state, eye).reshape(b, GROUP_W, GROUP_W)
    y, st = pl.pallas_call(
        functools.partial(_hgrn_kernel, tt=tt, chunk=chunk, sub=sub),
        grid=(b, t_len // tt),
        in_specs=[pl.BlockSpec((1, tt, 4 * GROUP_W), lambda i, j: (i, j, 0)),
                  pl.BlockSpec((1, GROUP_W, GROUP_W), lambda i, j: (i, 0, 0)),
                  pl.BlockSpec((1, GROUP_W), lambda i, j: (0, 0)),
                  pl.BlockSpec((1, GROUP_W), lambda i, j: (0, 0))],
        out_specs=[pl.BlockSpec((1, tt, GROUP_W), lambda i, j: (i, j, 0)),
                   pl.BlockSpec((1, GROUP_W, GROUP_W), lambda i, j: (i, 0, 0))],
        out_shape=[jax.ShapeDtypeStruct((b, t_len, GROUP_W), F32),
                   jax.ShapeDtypeStruct((b, GROUP_W, GROUP_W), F32)],
        scratch_shapes=[pltpu.VMEM((GROUP_W, GROUP_W), F32)],
        compiler_params=_params(("parallel", "arbitrary")),
        name="hgrn",
    )(x4, st0, lb.reshape(1, GROUP_W), jnp.tile(norm_g, N_HEADS).reshape(1, GROUP_W))
    st5 = st.reshape(b, N_HEADS, HEAD_DIM, N_HEADS, HEAD_DIM)
    new_state = jnp.stack([st5[:, h, :, h, :] for h in range(N_HEADS)], axis=1)
    return y, jnp.swapaxes(new_state, 2, 3)


def _trunk(x, q_pos0, pool_st, conv_st, hgrn_st, cache_k, cache_v, page_table, w):
    b, t_len, d = x.shape
    n = b * t_len
    depth = w["ln_ffn1"].shape[0]
    lbs = jax.nn.softmax(w["hgrn_lower_bounds"].astype(F32), axis=0)
    lbs = jnp.cumsum(lbs, axis=0) - lbs[0]
    prefill = cache_k is None
    xf = x.reshape(n, d)
    ks, vs, pools, convs, hgrns = [], [], [], [], []
    for l in range(depth):
        xf = _ffn(xf, w["ln_ffn1"][l], w["ffn1_w_gate"][l], w["ffn1_w_up"][l], w["ffn1_w_down"][l])
        outs = _inproj(xf, t_len, q_pos0, w["ln_mix"][l], w["w_in"][l], w["q_norm"][l], w["k_norm"][l], prefill)
        up, q, k, v, kb, vb, cu, x4 = outs[:8]
        seq = lambda a: a.reshape(b, t_len, a.shape[-1])
        ya, yc, pool_new, conv_new = _poolconv(
            seq(up), seq(cu), pool_st[l], conv_st[l], q_pos0, w["pool_w"][l], w["pool_scale"][l],
            w["conv_w"][l], w["conv_b"][l], w["conv_ln_g"][l], w["conv_ln_b"][l], w["conv_pw"][l])
        if prefill:
            kmean = outs[8].reshape(b, t_len // MOBA_BLOCK, 1, GROUP_W)
            yb = _moba_prefill(seq(q), seq(kb), seq(vb), kmean)
        else:
            yb = _moba_decode(seq(q), seq(k), seq(v), cache_k, cache_v, page_table, l)
        yd, s_new = _hgrn(seq(x4), hgrn_st[l], lbs[l], w["hgrn_norm"][l])
        flat = lambda a: a.reshape(n, GROUP_W)
        xf = _ffn(xf, w["ln_ffn2"][l], w["ffn2_w_gate"][l], w["ffn2_w_up"][l], w["ffn2_w_down"][l],
                  mix=(flat(ya), flat(yb), flat(yc), flat(yd)), w_out=w["w_out"][l])
        ks.append(k.reshape(b, t_len, N_HEADS, HEAD_DIM))
        vs.append(v.reshape(b, t_len, N_HEADS, HEAD_DIM))
        pools.append(pool_new)
        convs.append(conv_new)
        hgrns.append(s_new)
    return (xf.reshape(b, t_len, d), jnp.stack(ks), jnp.stack(vs), jnp.stack(pools), jnp.stack(convs),
            jnp.stack(hgrns))


def kernel(x_prompt, x_sample, cache_k, cache_v, page_table, state_pool, state_conv, state_hgrn, ln_ffn1, ffn1_w_gate, ffn1_w_up, ffn1_w_down, ln_mix, w_in, w_out, pool_w, pool_scale, q_norm, k_norm, conv_w, conv_b, conv_ln_g, conv_ln_b, conv_pw, hgrn_lower_bounds, hgrn_norm, ln_ffn2, ffn2_w_gate, ffn2_w_up, ffn2_w_down):
    w = dict(ln_ffn1=ln_ffn1, ln_mix=ln_mix, ln_ffn2=ln_ffn2, pool_w=pool_w, pool_scale=pool_scale,
             q_norm=q_norm, k_norm=k_norm, conv_w=conv_w, conv_b=conv_b, conv_ln_g=conv_ln_g,
             conv_ln_b=conv_ln_b, conv_pw=conv_pw, hgrn_lower_bounds=hgrn_lower_bounds, hgrn_norm=hgrn_norm)
    for name, arr in (("ffn1_w_gate", ffn1_w_gate), ("ffn1_w_up", ffn1_w_up), ("ffn1_w_down", ffn1_w_down),
                      ("w_in", w_in), ("w_out", w_out), ("ffn2_w_gate", ffn2_w_gate), ("ffn2_w_up", ffn2_w_up),
                      ("ffn2_w_down", ffn2_w_down)):
        w[name] = arr.astype(BF16)
    depth = ln_ffn1.shape[0]
    bp = x_prompt.shape[0]
    dt = x_prompt.dtype
    zero_pool = jnp.zeros((depth, bp) + state_pool.shape[2:], dt)
    zero_conv = jnp.zeros((depth, bp) + state_conv.shape[2:], dt)
    zero_hgrn = jnp.zeros((depth, bp) + state_hgrn.shape[2:], dt)
    y_p, k_p, v_p, pool_p, conv_p, hgrn_p = _trunk(
        x_prompt, 0, zero_pool, zero_conv, zero_hgrn, None, None, None, w)
    past_len = page_table.shape[1] * cache_k.shape[2]
    ck = cache_k.reshape(cache_k.shape[:3] + (GROUP_W,))
    cv = cache_v.reshape(cache_v.shape[:3] + (GROUP_W,))
    y_s, k_s, v_s, pool_s, conv_s, hgrn_s = _trunk(
        x_sample, past_len, state_pool, state_conv, state_hgrn, ck, cv, page_table, w)
    return (y_p, y_s, k_p, v_p, k_s, v_s, pool_p, pool_s, conv_p, conv_s, hgrn_p, hgrn_s)
```

```python
import functools

import jax
import jax.numpy as jnp
from jax import lax
from jax.experimental import pallas as pl
from jax.experimental.pallas import tpu as pltpu

F32 = jnp.float32
BF16 = jnp.bfloat16
HI = lax.Precision.HIGHEST

NORM_EPS = 1e-6
GROUP_W = 256
N_HEADS = 4
HEAD_DIM = GROUP_W // N_HEADS
N_IN_SPLITS = 10
POOL_WINDOWS = (2, 4, 8, 16)
POOL_PAD = 16
CONV_W = 31
CONV_PAD = 32
MOBA_BLOCK = 256
MOBA_TOPK = 3
ROPE_DIM = HEAD_DIM // 4
ROPE_THETA = 500000.0
ATTN_SCALE = HEAD_DIM ** -0.5
LOG2_E = 1.4426950408889634
NEG_BIG = -1e30
HGRN_CHUNK = 64
HGRN_SUB = 8
PAGES_PER_CHUNK = 16
VMEM_LIMIT = 56 * 1024 * 1024


def _params(sem=None):
    return pltpu.CompilerParams(dimension_semantics=sem, vmem_limit_bytes=VMEM_LIMIT)


def _dot(a, b):
    return jnp.dot(a, b, preferred_element_type=F32)


def _dot_nt(a, b, precision=None):
    return lax.dot_general(a, b, (((1,), (1,)), ((), ())), precision=precision,
                           preferred_element_type=F32)


def _lane_head(shape):
    return lax.broadcasted_iota(jnp.int32, shape, len(shape) - 1) // HEAD_DIM


def _head_ones():
    r = lax.broadcasted_iota(jnp.int32, (GROUP_W, GROUP_W), 0) // HEAD_DIM
    c = lax.broadcasted_iota(jnp.int32, (GROUP_W, GROUP_W), 1) // HEAD_DIM
    return r == c


def _sigmoid(x):
    return 1.0 / (1.0 + jnp.exp(-x))


def _ffn_kernel(*refs, n_mix, ff_chunk):
    x_ref = refs[0]
    mix_refs = refs[1:1 + 2 * n_mix]
    g_ref, wg_ref, wu_ref, wd_ref, o_ref = refs[1 + 2 * n_mix:]
    x = x_ref[...]
    for idx in range(n_mix):
        x = x + _dot(mix_refs[idx][...].astype(BF16), mix_refs[n_mix + idx][...])
    ms = jnp.mean(x * x, axis=-1, keepdims=True)
    h = (x * lax.rsqrt(ms + NORM_EPS) * g_ref[...]).astype(BF16)
    acc = None
    for c0 in range(0, wg_ref.shape[1], ff_chunk):
        g = _dot(h, wg_ref[:, c0:c0 + ff_chunk])
        u = _dot(h, wu_ref[:, c0:c0 + ff_chunk])
        a = (g * _sigmoid(g) * u).astype(BF16)
        part = _dot(a, wd_ref[c0:c0 + ff_chunk, :])
        acc = part if acc is None else acc + part
    o_ref[...] = x + 0.5 * acc


def _resident(shape):
    return pl.BlockSpec(shape, lambda i: (0,) * len(shape), pipeline_mode=pl.Buffered(1))


def _ffn(x, gain, wg, wu, wd, mix=()):
    n, d = x.shape
    d_ff = wg.shape[1]
    tm = 512 if n % 512 == 0 else n
    ff_chunk = d_ff // 2 if d_ff % 256 == 0 else d_ff
    tok = lambda i: (i, 0)
    in_specs = [pl.BlockSpec((tm, d), tok)]
    in_specs += [pl.BlockSpec((tm, y.shape[1]), tok) for y, _ in mix]
    in_specs += [_resident(w.shape) for _, w in mix]
    in_specs += [_resident((1, d)), _resident(wg.shape), _resident(wu.shape), _resident(wd.shape)]
    args = [x] + [y for y, _ in mix] + [w for _, w in mix] + [gain.reshape(1, d), wg, wu, wd]
    return pl.pallas_call(
        functools.partial(_ffn_kernel, n_mix=len(mix), ff_chunk=ff_chunk),
        grid=(n // tm,),
        in_specs=in_specs,
        out_specs=pl.BlockSpec((tm, d), tok),
        out_shape=jax.ShapeDtypeStruct((n, d), F32),
        compiler_params=_params(("parallel",)),
        name="ffn_merge" if mix else "ffn",
    )(*args)


def _spread_heads(x, fill):
    lane = lax.broadcasted_iota(jnp.int32, (x.shape[0], 128), 1)
    odd = pltpu.roll(x, GROUP_W - HEAD_DIM, axis=1)
    parts = (x[:, :128], odd[:, :128], x[:, 128:], odd[:, 128:])
    return jnp.concatenate([jnp.where(lane < HEAD_DIM, p, fill) for p in parts], axis=1)


def _inproj_kernel(*refs, prefill, tm, per_seq):
    x_ref, g_ref, w_ref, qg_ref, kg_ref, cos_ref, sin_ref = refs[:7]
    if prefill:
        up_ref, k_ref, v_ref, qp_ref, ka_ref, va_ref, cu_ref, hg_ref, km_ref = refs[7:]
    else:
        up_ref, q_ref, k_ref, v_ref, cu_ref, hg_ref = refs[7:]
    x = x_ref[...]
    ms = jnp.mean(x * x, axis=-1, keepdims=True)
    h = (x * lax.rsqrt(ms + NORM_EPS) * g_ref[...]).astype(BF16)

    def proj(c):
        return _dot(h, w_ref[:, c * GROUP_W:(c + 1) * GROUP_W])

    ones = _head_ones().astype(F32)
    cos = cos_ref[...]
    sin = sin_ref[...]
    d = lax.broadcasted_iota(jnp.int32, cos.shape, 1) % HEAD_DIM
    half = ROPE_DIM // 2

    def norm_rope(p, gain):
        ms_h = jnp.dot(p * p, ones, precision=HI, preferred_element_type=F32) * (1.0 / HEAD_DIM)
        y = p * lax.rsqrt(ms_h + NORM_EPS) * gain
        partner = jnp.where(d < half, pltpu.roll(y, GROUP_W - half, axis=1), pltpu.roll(y, half, axis=1))
        return y * cos + partner * sin

    up_ref[...] = proj(0)
    q = norm_rope(proj(1), qg_ref[...])
    k = norm_rope(proj(2), kg_ref[...])
    v = proj(3)
    k_ref[...] = k
    v_ref[...] = v
    cu_ref[...] = proj(4) * _sigmoid(proj(5))
    for c in range(4):
        hg_ref[:, c * GROUP_W:(c + 1) * GROUP_W] = proj(6 + c)
    if not prefill:
        q_ref[...] = q
        return
    lane = lax.broadcasted_iota(jnp.int32, (tm, 128), 1)
    row = lax.broadcasted_iota(jnp.int32, (tm, 128), 0)
    blk = ((pl.program_id(0) % per_seq) * tm + row) // MOBA_BLOCK
    zero = jnp.zeros((tm, 128), F32)
    kp = _spread_heads(k, zero)
    qp_ref[...] = _spread_heads(q, zero)
    ka_ref[...] = _spread_heads(k, jnp.where(lane - HEAD_DIM == blk, 1.0, 0.0)).astype(BF16)
    va_ref[...] = _spread_heads(v, jnp.where(lane == HEAD_DIM, 1.0, 0.0)).astype(BF16)
    for n in range(km_ref.shape[0]):
        km_ref[n] = jnp.mean(kp[n * MOBA_BLOCK:(n + 1) * MOBA_BLOCK, :], axis=0, keepdims=True)


def _rope_tables(t_len, q_pos0):
    half = ROPE_DIM // 2
    inv = ROPE_THETA ** (-jnp.arange(half, dtype=F32) * 2.0 / ROPE_DIM)
    pos = q_pos0 + jnp.arange(t_len, dtype=jnp.int32)
    ang = pos.astype(F32)[:, None] * inv
    c, s = jnp.cos(ang), jnp.sin(ang)
    rest = HEAD_DIM - ROPE_DIM
    cos_h = jnp.concatenate([c, c, jnp.ones((t_len, rest), F32)], axis=1)
    sin_h = jnp.concatenate([-s, s, jnp.zeros((t_len, rest), F32)], axis=1)
    return jnp.tile(cos_h, (1, N_HEADS)), jnp.tile(sin_h, (1, N_HEADS))


def _inproj(x, t_len, q_pos0, gain, w_in, q_gain, k_gain, prefill):
    n, d = x.shape
    if t_len % 512 == 0:
        tm = 512
    else:
        tm = t_len * max(r for r in range(1, n // t_len + 1) if (n // t_len) % r == 0 and t_len * r <= 512)
    assert not prefill or (tm % MOBA_BLOCK == 0 and t_len // MOBA_BLOCK <= 128 - HEAD_DIM)
    cos_t, sin_t = _rope_tables(t_len, q_pos0)
    if tm > t_len:
        cos_t, sin_t = jnp.tile(cos_t, (tm // t_len, 1)), jnp.tile(sin_t, (tm // t_len, 1))
    per_seq = max(t_len // tm, 1)
    tok = lambda i: (i, 0)
    tab = lambda i: (i % per_seq, 0)
    wide = 2 * GROUP_W
    grp = jax.ShapeDtypeStruct((n, GROUP_W), F32)
    gspec = pl.BlockSpec((tm, GROUP_W), tok)
    wspec = pl.BlockSpec((tm, wide), tok)
    x4 = jax.ShapeDtypeStruct((n, 4 * GROUP_W), F32)
    x4spec = pl.BlockSpec((tm, 4 * GROUP_W), tok)
    if prefill:
        out_shape = [grp, grp, grp, jax.ShapeDtypeStruct((n, wide), F32), jax.ShapeDtypeStruct((n, wide), BF16),
                     jax.ShapeDtypeStruct((n, wide), BF16), grp, x4,
                     jax.ShapeDtypeStruct((n // MOBA_BLOCK, 1, wide), F32)]
        out_specs = [gspec, gspec, gspec, wspec, wspec, wspec, gspec, x4spec,
                     pl.BlockSpec((tm // MOBA_BLOCK, 1, wide), lambda i: (i, 0, 0))]
    else:
        out_shape = [grp, grp, grp, grp, grp, x4]
        out_specs = [gspec] * 5 + [x4spec]
    return pl.pallas_call(
        functools.partial(_inproj_kernel, prefill=prefill, tm=tm, per_seq=per_seq),
        grid=(n // tm,),
        in_specs=[pl.BlockSpec((tm, d), tok), _resident((1, d)), _resident(w_in.shape),
                  _resident((1, GROUP_W)), _resident((1, GROUP_W)),
                  pl.BlockSpec((tm, GROUP_W), tab), pl.BlockSpec((tm, GROUP_W), tab)],
        out_specs=out_specs,
        out_shape=out_shape,
        compiler_params=_params(("parallel",)),
        name="inproj",
    )(x, gain.reshape(1, d), w_in, jnp.tile(q_gain, N_HEADS).reshape(1, GROUP_W),
      jnp.tile(k_gain, N_HEADS).reshape(1, GROUP_W), cos_t, sin_t)


def _poolconv_kernel(up_ref, cu_ref, pprev_ref, cprev_ref, pw_ref, ps_ref, cw_ref, cb_ref, lg_ref, lb_ref, cpw_ref,
                     ya_ref, yc_ref, pnew_ref, cnew_ref, pbuf, cbuf, *, tt, q_pos0, rows):
    t = pl.program_id(1)

    @pl.when(t == 0)
    def _():
        pbuf[0:POOL_PAD, :] = pprev_ref[0]
        cbuf[0:CONV_PAD, :] = cprev_ref[0]

    pbuf[POOL_PAD:POOL_PAD + tt, :] = up_ref[0]
    cbuf[CONV_PAD:CONV_PAD + tt, :] = cu_ref[0]

    for r0 in range(0, tt, rows):
        pos1 = (q_pos0 + t * tt + r0 + 1 + lax.broadcasted_iota(jnp.int32, (rows, 128), 0)).astype(F32)
        first = lax.broadcasted_iota(jnp.int32, (rows, 128), 1) < (GROUP_W // 4)
        pooled = []
        for half_i, (w_a, w_b) in enumerate(((POOL_WINDOWS[0], POOL_WINDOWS[1]), (POOL_WINDOWS[2], POOL_WINDOWS[3]))):
            lanes = slice(half_i * 128, (half_i + 1) * 128)
            u = pbuf[POOL_PAD + r0:POOL_PAD + r0 + rows, lanes]
            acc = u
            s_a = None
            for j in range(1, w_b):
                acc = acc + pbuf[POOL_PAD + r0 - j:POOL_PAD + r0 - j + rows, lanes]
                if j == w_a - 1:
                    s_a = acc
            avg = jnp.where(first, s_a / jnp.minimum(float(w_a), pos1), acc / jnp.minimum(float(w_b), pos1))
            pooled.append((avg - u).astype(BF16))
        pooled = jnp.concatenate(pooled, axis=1)
        ya_ref[0, r0:r0 + rows, :] = _dot(pooled, pw_ref[...]) * ps_ref[...]

        acc = cbuf[CONV_PAD - (CONV_W - 1) + r0:CONV_PAD - (CONV_W - 1) + r0 + rows, :] * cw_ref[0:1, :]
        for j in range(1, CONV_W):
            base = CONV_PAD - (CONV_W - 1) + j + r0
            acc = acc + cbuf[base:base + rows, :] * cw_ref[j:j + 1, :]
        y = acc + cb_ref[...]
        mu = jnp.mean(y, axis=-1, keepdims=True)
        var = jnp.mean(jnp.square(y - mu), axis=-1, keepdims=True)
        y = (y - mu) * lax.rsqrt(var + NORM_EPS) * lg_ref[...] + lb_ref[...]
        y = y * _sigmoid(y)
        yc_ref[0, r0:r0 + rows, :] = _dot(y.astype(BF16), cpw_ref[...])

    ptail = pbuf[tt:tt + POOL_PAD, :]
    ctail = cbuf[tt:tt + CONV_PAD, :]
    pnew_ref[0] = ptail
    cnew_ref[0] = ctail
    pbuf[0:POOL_PAD, :] = ptail
    cbuf[0:CONV_PAD, :] = ctail


def _poolconv(up, cu, pool_prev, conv_prev, q_pos0, pool_w, pool_scale, conv_w, conv_b, ln_g, ln_b, conv_pw):
    b, t_len, _ = up.shape
    tt = 512 if t_len % 512 == 0 else t_len
    rows = 64 if tt % 64 == 0 else tt
    pprev = jnp.pad(pool_prev, ((0, 0), (POOL_PAD - pool_prev.shape[1], 0), (0, 0)))
    cprev = jnp.pad(conv_prev, ((0, 0), (CONV_PAD - conv_prev.shape[1], 0), (0, 0)))
    pw_bd = jax.scipy.linalg.block_diag(*[pool_w[g] for g in range(pool_w.shape[0])]).astype(BF16)
    cw_pad = jnp.pad(conv_w, ((0, CONV_PAD - CONV_W), (0, 0)))
    seq = lambda i, j: (i, j, 0)
    per_b = lambda i, j: (i, 0, 0)
    const = lambda i, j: (0, 0)
    row = pl.BlockSpec((1, GROUP_W), const)
    ya, yc, pnew, cnew = pl.pallas_call(
        functools.partial(_poolconv_kernel, tt=tt, q_pos0=q_pos0, rows=rows),
        grid=(b, t_len // tt),
        in_specs=[pl.BlockSpec((1, tt, GROUP_W), seq), pl.BlockSpec((1, tt, GROUP_W), seq),
                  pl.BlockSpec((1, POOL_PAD, GROUP_W), per_b), pl.BlockSpec((1, CONV_PAD, GROUP_W), per_b),
                  pl.BlockSpec((GROUP_W, GROUP_W), const), row,
                  pl.BlockSpec((CONV_PAD, GROUP_W), const), row, row, row,
                  pl.BlockSpec((GROUP_W, GROUP_W), const)],
        out_specs=[pl.BlockSpec((1, tt, GROUP_W), seq), pl.BlockSpec((1, tt, GROUP_W), seq),
                   pl.BlockSpec((1, POOL_PAD, GROUP_W), per_b), pl.BlockSpec((1, CONV_PAD, GROUP_W), per_b)],
        out_shape=[jax.ShapeDtypeStruct((b, t_len, GROUP_W), F32), jax.ShapeDtypeStruct((b, t_len, GROUP_W), F32),
                   jax.ShapeDtypeStruct((b, POOL_PAD, GROUP_W), F32), jax.ShapeDtypeStruct((b, CONV_PAD, GROUP_W), F32)],
        scratch_shapes=[pltpu.VMEM((POOL_PAD + tt, GROUP_W), F32), pltpu.VMEM((CONV_PAD + tt, GROUP_W), F32)],
        compiler_params=_params(("parallel", "arbitrary")),
        name="poolconv",
    )(up, cu, pprev, cprev, pw_bd, pool_scale.reshape(1, GROUP_W), cw_pad, conv_b.reshape(1, GROUP_W),
      ln_g.reshape(1, GROUP_W), ln_b.reshape(1, GROUP_W), conv_pw.astype(BF16))
    return ya, yc, pnew[:, POOL_PAD - pool_prev.shape[1]:], cnew[:, CONV_PAD - conv_prev.shape[1]:]


def _stack_heads(q):
    head = _lane_head(q.shape)
    return jnp.concatenate([jnp.where(head == h, q, 0.0) for h in range(N_HEADS)], axis=0)


def _select_blocks(gate, n_past):
    lane = lax.broadcasted_iota(jnp.int32, gate.shape, 1)
    past = lane < n_past
    g = jnp.where(past, gate, NEG_BIG)
    sel = jnp.zeros(gate.shape, jnp.bool_)
    for _ in range(MOBA_TOPK):
        mx = jnp.max(g, axis=1, keepdims=True)
        idx = jnp.min(jnp.where(g == mx, lane, 1 << 20), axis=1, keepdims=True)
        pick = lane == idx
        sel = jnp.logical_or(sel, pick)
        g = jnp.where(pick, -jnp.inf, g)
    return jnp.logical_and(sel, past)


def _select_block_rows(gate_t, n_past, nblk):
    n = lax.broadcasted_iota(jnp.int32, gate_t.shape, 0)
    past = n < n_past
    g = jnp.where(past, gate_t, jnp.where(n < nblk, NEG_BIG, -jnp.inf))
    sel = jnp.zeros(gate_t.shape, jnp.bool_)
    for _ in range(MOBA_TOPK):
        mx = jnp.max(g, axis=0, keepdims=True)
        idx = jnp.min(jnp.where(g == mx, n, 1 << 20), axis=0, keepdims=True)
        pick = n == idx
        sel = jnp.logical_or(sel, pick)
        g = jnp.where(pick, -jnp.inf, g)
    return jnp.logical_and(sel, past)


def _moba_prefill_kernel(qp_ref, ka_ref, va_ref, km_ref, o_ref,
                         kmr_ref, qa_ref, s_ref, mx_ref, m_ref, acc_ref, *, nblk):
    i = pl.program_id(1)
    blk = MOBA_BLOCK
    kb = kmr_ref.shape[1]
    heads = [slice(h * 128, (h + 1) * 128) for h in range(N_HEADS)]

    @pl.when(i == 0)
    def _():
        kmr_ref[...] = jnp.zeros_like(kmr_ref)
        for h in range(N_HEADS):
            for n in range(nblk):
                kmr_ref[h, n:n + 1, :] = km_ref[0, n][:, heads[h]]

    n_row = lax.broadcasted_iota(jnp.int32, (kb, blk), 0)
    for h in range(N_HEADS):
        qh = qp_ref[0, :, heads[h]]
        gate_t = _dot_nt(kmr_ref[h], qh, precision=HI)
        sel = _select_block_rows(gate_t, i, nblk)
        bias_t = jnp.where(jnp.logical_or(sel, n_row == i), 0.0, NEG_BIG)
        bias_t = jnp.concatenate([jnp.zeros((HEAD_DIM, blk), F32), bias_t,
                                  jnp.zeros((128 - HEAD_DIM - kb, blk), F32)], axis=0)
        qa_ref[h] = (qh * (ATTN_SCALE * LOG2_E) + bias_t.T).astype(BF16)
        mx_ref[h] = jnp.full((blk, 128), NEG_BIG, F32)
        acc_ref[h] = jnp.zeros((blk, 128), F32)

    row = lax.broadcasted_iota(jnp.int32, (blk, blk), 0)
    col = lax.broadcasted_iota(jnp.int32, (blk, blk), 1)
    n_pairs = (i + 2) // 2

    def scores(jj, carry):
        for u in range(2):
            j = 2 * jj + u
            j0 = pl.multiple_of(j * blk, blk)
            allow = col + (j - i) * blk <= row
            for h in range(N_HEADS):
                s = jnp.where(allow, _dot_nt(qa_ref[h], ka_ref[0, pl.ds(j0, blk), heads[h]]), NEG_BIG)
                s_ref[h, j] = s
                mx_ref[h] = jnp.maximum(mx_ref[h], jnp.maximum(s[:, :128], s[:, 128:]))
        return carry

    lax.fori_loop(0, n_pairs, scores, 0)
    for h in range(N_HEADS):
        m_ref[h] = jnp.broadcast_to(jnp.max(mx_ref[h], axis=1, keepdims=True), (blk, 128))

    def weighted(jj, carry):
        for u in range(2):
            j = 2 * jj + u
            j0 = pl.multiple_of(j * blk, blk)
            for h in range(N_HEADS):
                s = s_ref[h, j]
                m = m_ref[h]
                p = jnp.concatenate([jnp.exp2(s[:, :128] - m), jnp.exp2(s[:, 128:] - m)], axis=1).astype(BF16)
                acc_ref[h] += _dot(p, va_ref[0, pl.ds(j0, blk), heads[h]])
        return carry

    lax.fori_loop(0, n_pairs, weighted, 0)
    lane = lax.broadcasted_iota(jnp.int32, (blk, 128), 1)
    for h in range(N_HEADS):
        acc = acc_ref[h]
        o_ref[0, :, heads[h]] = jnp.where(lane < HEAD_DIM, acc / acc[:, HEAD_DIM:HEAD_DIM + 1], 0.0)


def _moba_prefill(qp, ka, va, kmean):
    b, t_len, wide = qp.shape
    nblk = t_len // MOBA_BLOCK
    assert nblk % 2 == 0 and nblk <= 128 - HEAD_DIM
    kb = -(-nblk // 8) * 8
    blk = MOBA_BLOCK
    per_b = lambda i, j: (i, 0, 0)
    return pl.pallas_call(
        functools.partial(_moba_prefill_kernel, nblk=nblk),
        grid=(b, nblk),
        in_specs=[pl.BlockSpec((1, blk, wide), lambda i, j: (i, j, 0)),
                  pl.BlockSpec((1, t_len, wide), per_b, pipeline_mode=pl.Buffered(1)),
                  pl.BlockSpec((1, t_len, wide), per_b, pipeline_mode=pl.Buffered(1)),
                  pl.BlockSpec((1, nblk, 1, wide), lambda i, j: (i, 0, 0, 0))],
        out_specs=pl.BlockSpec((1, blk, wide), lambda i, j: (i, j, 0)),
        out_shape=jax.ShapeDtypeStruct((b, t_len, wide), F32),
        scratch_shapes=[pltpu.VMEM((N_HEADS, kb, 128), F32), pltpu.VMEM((N_HEADS, blk, 128), BF16),
                        pltpu.VMEM((N_HEADS, nblk, blk, blk), F32), pltpu.VMEM((N_HEADS, blk, 128), F32),
                        pltpu.VMEM((N_HEADS, blk, 128), F32), pltpu.VMEM((N_HEADS, blk, 128), F32)],
        compiler_params=_params(("parallel", "arbitrary")),
        name="moba_prefill",
    )(qp, ka, va, kmean)


def _moba_decode_kernel(pt_ref, q_ref, kn_ref, vn_ref, ck_ref, cv_ref, o_ref,
                        buf, sem, s_ref, p_ref, own_ref, *, layer, n_pages, tq):
    b = pl.program_id(0)
    ch = PAGES_PER_CHUNK
    n_chunks = n_pages // ch
    page = buf.shape[3]
    pages_per_blk = MOBA_BLOCK // page
    n_past = n_pages // pages_per_blk
    rows = N_HEADS * tq

    def page_copy(cache_ref, c, pg, slot):
        return pltpu.make_async_copy(cache_ref.at[layer, pt_ref[b, c * ch + pg]], buf.at[slot, pg], sem.at[slot])

    def start_chunk(cache_ref, c, slot):
        for pg in range(ch):
            page_copy(cache_ref, c, pg, slot).start()

    def wait_chunk(cache_ref, c, slot):
        for pg in range(ch):
            page_copy(cache_ref, c, pg, slot).wait()

    q4 = _stack_heads(q_ref[0]) * ATTN_SCALE
    q_hi = q4.astype(BF16)
    qq = jnp.concatenate([q_hi, (q4 - q_hi.astype(F32)).astype(BF16)], axis=0)

    def both(s2):
        return s2[:rows] + s2[rows:]

    start_chunk(ck_ref, 0, 0)

    def k_chunk(c, carry):
        slot = c % 2

        @pl.when(c + 1 < n_chunks)
        def _():
            start_chunk(ck_ref, c + 1, 1 - slot)

        @pl.when(c + 1 == n_chunks)
        def _():
            start_chunk(cv_ref, 0, 1 - slot)

        wait_chunk(ck_ref, c, slot)
        for pg in range(ch):
            s_ref[c * ch + pg] = both(_dot(qq, buf[slot, pg].astype(BF16)))
        return carry

    lax.fori_loop(0, n_chunks, k_chunk, 0)

    lane = lax.broadcasted_iota(jnp.int32, (rows, 128), 1)
    gate = jnp.zeros((rows, 128), F32)
    for n in range(n_past):
        tot = s_ref[n * pages_per_blk]
        for pg in range(1, pages_per_blk):
            tot = tot + s_ref[n * pages_per_blk + pg]
        gate = jnp.where(lane == n, jnp.sum(tot, axis=1, keepdims=True), gate)
    gate = gate * (1.0 / (MOBA_BLOCK * ATTN_SCALE))
    sel = _select_blocks(gate, n_past).astype(F32)

    own_ref[...] = jnp.zeros_like(own_ref)
    own_ref[0, 0:tq, :] = kn_ref[0]
    own_ref[1, 0:tq, :] = vn_ref[0]
    s_own = both(_dot_nt(qq, own_ref[0].astype(BF16)))
    s_own = jnp.where(lane <= lax.broadcasted_iota(jnp.int32, (rows, 128), 0) % tq, s_own, NEG_BIG)

    mx = s_own
    for n in range(n_past):
        ok = sel[:, n:n + 1] > 0.5
        for pg in range(n * pages_per_blk, (n + 1) * pages_per_blk):
            s_pg = jnp.where(ok, s_ref[pg], NEG_BIG)
            s_ref[pg] = s_pg
            mx = jnp.maximum(mx, s_pg)
    m = jnp.max(mx, axis=1, keepdims=True)
    p_own = jnp.exp(s_own - m)
    lsum = p_own
    for pg in range(n_pages):
        p = jnp.exp(s_ref[pg] - m)
        lsum = lsum + p
        p_ref[pg] = p.astype(BF16)
    l = jnp.sum(lsum, axis=1, keepdims=True)

    def v_chunk(c, acc):
        slot = (n_chunks + c) % 2

        @pl.when(c + 1 < n_chunks)
        def _():
            start_chunk(cv_ref, c + 1, 1 - slot)

        wait_chunk(cv_ref, c, slot)
        for pg in range(ch):
            acc = acc + _dot_nt(p_ref[c * ch + pg], buf[slot, pg].astype(BF16))
        return acc

    acc = lax.fori_loop(0, n_chunks, v_chunk, _dot(p_own.astype(BF16), own_ref[1].astype(BF16)))
    acc = acc / l
    head = _lane_head((tq, GROUP_W))
    out = jnp.zeros((tq, GROUP_W), F32)
    for h in range(N_HEADS):
        out = out + jnp.where(head == h, acc[h * tq:(h + 1) * tq, :], 0.0)
    o_ref[0] = out


def _moba_decode(q, k_new, v_new, cache_k, cache_v, page_table, layer):
    b, tq, _ = q.shape
    n_pages = page_table.shape[1]
    page = cache_k.shape[3]
    assert page == 128 and MOBA_BLOCK % page == 0 and n_pages % (MOBA_BLOCK // page) == 0
    assert n_pages % PAGES_PER_CHUNK == 0 and tq <= 128 and n_pages * page // MOBA_BLOCK <= 128
    rows = N_HEADS * tq
    per_b = lambda i, pt: (i, 0, 0)
    grid_spec = pltpu.PrefetchScalarGridSpec(
        num_scalar_prefetch=1,
        grid=(b,),
        in_specs=[pl.BlockSpec((1, tq, GROUP_W), per_b), pl.BlockSpec((1, tq, GROUP_W), per_b),
                  pl.BlockSpec((1, tq, GROUP_W), per_b),
                  pl.BlockSpec(memory_space=pl.ANY), pl.BlockSpec(memory_space=pl.ANY)],
        out_specs=pl.BlockSpec((1, tq, GROUP_W), per_b),
        scratch_shapes=[pltpu.VMEM((2, PAGES_PER_CHUNK, GROUP_W, page), F32),
                        pltpu.SemaphoreType.DMA((2,)),
                        pltpu.VMEM((n_pages, rows, page), F32), pltpu.VMEM((n_pages, rows, page), BF16),
                        pltpu.VMEM((2, 128, GROUP_W), F32)],
    )
    return pl.pallas_call(
        functools.partial(_moba_decode_kernel, layer=layer, n_pages=n_pages, tq=tq),
        grid_spec=grid_spec,
        out_shape=jax.ShapeDtypeStruct((b, tq, GROUP_W), F32),
        compiler_params=_params(("arbitrary",)),
        name="moba_decode",
    )(page_table, q, k_new, v_new, cache_k, cache_v)


def _hgrn_kernel(x_ref, st0_ref, lb_ref, ng_ref, y_ref, stout_ref, st_ref, qst_ref, upd_ref, dec_ref,
                 *, tt, chunk, sub):
    t = pl.program_id(1)
    n_sub = chunk // sub

    @pl.when(t == 0)
    def _():
        st_ref[...] = st0_ref[0]

    lb = lb_ref[...]
    head = _lane_head((1, GROUP_W))
    hmask = [head == h for h in range(N_HEADS)]
    bd = _head_ones()
    ones_b = bd.astype(BF16)
    ones_f = bd.astype(F32)
    tri = (lax.broadcasted_iota(jnp.int32, (chunk, chunk), 1)
           <= lax.broadcasted_iota(jnp.int32, (chunk, chunk), 0)).astype(F32)
    t_ge = [lax.broadcasted_iota(jnp.int32, (sub, GROUP_W), 0) >= s for s in range(sub)]

    n_chunks = tt // chunk
    if n_sub > 1:
        n_used = sub * (n_sub * (n_sub - 1) // 2)
        n_col = -(-n_used // 128) * 128
        col =lax.broadcasted_iota(jnp.int32, (N_HEADS * chunk, n_col), 1)
        row_sub = (lax.broadcasted_iota(jnp.int32, (N_HEADS * chunk, n_col), 0) % chunk) // sub
        keep = jnp.zeros((N_HEADS * chunk, n_col), jnp.bool_)
        c0 = 0
        for i in range(1, n_sub):
            keep = jnp.logical_or(keep, jnp.logical_and(row_sub == i, jnp.logical_and(col >= c0, col < c0 + i * sub)))
            c0 += i * sub

    def local_part(ci):
        r0 = ci * chunk if isinstance(ci, int) else pl.multiple_of(ci * chunk, chunk)
        hq = x_ref[0, pl.ds(r0, chunk), 0 * GROUP_W:1 * GROUP_W]
        hf = x_ref[0, pl.ds(r0, chunk), 1 * GROUP_W:2 * GROUP_W]
        hi = x_ref[0, pl.ds(r0, chunk), 2 * GROUP_W:3 * GROUP_W]
        f = lb + (1.0 - lb) * _sigmoid(hf)
        kin = 1.0 - f
        bc = jnp.dot(tri, jnp.log2(f), precision=HI, preferred_element_type=F32)
        b_last = bc[chunk - 1:chunk, :]
        qst_ref[pl.ds(r0, chunk), :] = hq * jnp.exp2(bc)
        k_state = (kin * jnp.exp2(b_last - bc)).astype(BF16)
        upd_ref[ci] = jnp.where(bd, _dot(hi.T.astype(BF16), k_state), 0.0)
        dec_ref[ci] = jnp.exp2(b_last)

        o = None
        if n_sub > 1:
            bq = jnp.concatenate(
                [jnp.zeros((sub, GROUP_W), F32)]
                + [jnp.broadcast_to(bc[i * sub - 1:i * sub, :], (sub, GROUP_W)) for i in range(1, n_sub)], axis=0)
            q_off = hq * jnp.exp2(bc - bq)
            q4 = jnp.concatenate([jnp.where(hmask[h], q_off, 0.0) for h in range(N_HEADS)], axis=0).astype(BF16)
            pad = [jnp.zeros((n_col - n_used, GROUP_W), F32)] if n_col > n_used else []
            k_st = jnp.concatenate(
                [kin[0:i * sub, :] * jnp.exp2(bc[i * sub - 1:i * sub, :] - bc[0:i * sub, :])
                 for i in range(1, n_sub)] + pad, axis=0).astype(BF16)
            v_st = jnp.concatenate([hi[0:i * sub, :] for i in range(1, n_sub)] + pad, axis=0).astype(BF16)
            att = jnp.where(keep, _dot_nt(q4, k_st), 0.0).astype(BF16)
            res = _dot(att, v_st)
            o = jnp.where(hmask[0], res[0:chunk], 0.0)
            for h in range(1, N_HEADS):
                o = o + jnp.where(hmask[h], res[h * chunk:(h + 1) * chunk], 0.0)

        d_rows = []
        for i in range(n_sub):
            rs = slice(i * sub, (i + 1) * sub)
            q_i, bc_i, k_i = hq[rs], bc[rs], kin[rs]
            for s in range(sub):
                e = jnp.exp2(jnp.where(t_ge[s], bc_i - bc_i[s:s + 1, :], NEG_BIG))
                d_rows.append(q_i * e * k_i[s:s + 1, :])
        a = _dot(jnp.concatenate(d_rows, axis=0).astype(BF16), ones_b)
        o_diag = []
        for i in range(n_sub):
            v_i = hi[i * sub:(i + 1) * sub]
            acc = a[i * sub * sub:i * sub * sub + sub, :] * v_i[0:1, :]
            for s in range(1, sub):
                base = (i * sub + s) * sub
                acc = acc + a[base:base + sub, :] * v_i[s:s + 1, :]
            o_diag.append(acc)
        o_diag = jnp.concatenate(o_diag, axis=0) if n_sub > 1 else o_diag[0]
        y_ref[0, pl.ds(r0, chunk), :] = o_diag if o is None else o + o_diag

    def carried_part(ci, carry):
        r0 = pl.multiple_of(ci * chunk, chunk)
        st = st_ref[...]
        y_ref[0, pl.ds(r0, chunk), :] += _dot_nt(qst_ref[pl.ds(r0, chunk), :].astype(BF16), st.astype(BF16))
        st_ref[...] = st * dec_ref[ci] + upd_ref[ci]
        return carry

    if n_chunks % 2 == 0:
        def pair(cp, carry):
            local_part(2 * cp)
            local_part(2 * cp + 1)
            return carry
        lax.fori_loop(0, n_chunks // 2, pair, 0)
    else:
        for ci in range(n_chunks):
            local_part(ci)
    lax.fori_loop(0, n_chunks, carried_part, 0)

    o = y_ref[0]
    hg = x_ref[0, :, 3 * GROUP_W:4 * GROUP_W]
    ms = jnp.dot(o * o, ones_f, precision=HI, preferred_element_type=F32) * (1.0 / HEAD_DIM)
    y_ref[0] = o * lax.rsqrt(ms + NORM_EPS) * ng_ref[...] * (hg * _sigmoid(hg))

    @pl.when(t == pl.num_programs(1) - 1)
    def _():
        stout_ref[0] = st_ref[...]


def _hgrn(x4, state, lb, norm_g):
    b, t_len, _ = x4.shape
    chunk = HGRN_CHUNK if t_len % HGRN_CHUNK == 0 else t_len
    sub = HGRN_SUB if chunk % HGRN_SUB == 0 else chunk
    tt = 512 if t_len % 512 == 0 else (chunk if t_len == chunk else t_len)
    eye = jnp.eye(N_HEADS, dtype=state.dtype)
    st0 = jnp.einsum('bhkv,hg->bhvgk', state, eye).reshape(b, GROUP_W, GROUP_W)
    y, st = pl.pallas_call(
        functools.partial(_hgrn_kernel, tt=tt, chunk=chunk, sub=sub),
        grid=(b, t_len // tt),
        in_specs=[pl.BlockSpec((1, tt, 4 * GROUP_W), lambda i, j: (i, j, 0)),
                  pl.BlockSpec((1, GROUP_W, GROUP_W), lambda i, j: (i, 0, 0)),
                  pl.BlockSpec((1, GROUP_W), lambda i, j: (0, 0)),
                  pl.BlockSpec((1, GROUP_W), lambda i, j: (0, 0))],
        out_specs=[pl.BlockSpec((1, tt, GROUP_W), lambda i, j: (i, j, 0)),
                   pl.BlockSpec((1, GROUP_W, GROUP_W), lambda i, j: (i, 0, 0))],
        out_shape=[jax.ShapeDtypeStruct((b, t_len, GROUP_W), F32),
                   jax.ShapeDtypeStruct((b, GROUP_W, GROUP_W), F32)],
        scratch_shapes=[pltpu.VMEM((GROUP_W, GROUP_W), F32), pltpu.VMEM((tt, GROUP_W), F32),
                        pltpu.VMEM((tt // chunk, GROUP_W, GROUP_W), F32), pltpu.VMEM((tt // chunk, 1, GROUP_W), F32)],
        compiler_params=_params(("parallel", "arbitrary")),
        name="hgrn",
    )(x4, st0, lb.reshape(1, GROUP_W), jnp.tile(norm_g, N_HEADS).reshape(1, GROUP_W))
    st5 = st.reshape(b, N_HEADS, HEAD_DIM, N_HEADS, HEAD_DIM)
    new_state = jnp.stack([st5[:, h, :, h, :] for h in range(N_HEADS)], axis=1)
    return y, jnp.swapaxes(new_state, 2, 3)


def _trunk(x, q_pos0, pool_st, conv_st, hgrn_st, cache_k, cache_v, page_table, w):
    b, t_len, d = x.shape
    n = b * t_len
    depth = w["ln_ffn1"].shape[0]
    lbs = jax.nn.softmax(w["hgrn_lower_bounds"].astype(F32), axis=0)
    lbs = jnp.cumsum(lbs, axis=0) - lbs[0]
    prefill = cache_k is None
    xf = x.reshape(n, d)
    ks, vs, pools, convs, hgrns = [], [], [], [], []
    for l in range(depth):
        xf = _ffn(xf, w["ln_ffn1"][l], w["ffn1_w_gate"][l], w["ffn1_w_up"][l], w["ffn1_w_down"][l])
        outs = _inproj(xf, t_len, q_pos0, w["ln_mix"][l], w["w_in"][l], w["q_norm"][l], w["k_norm"][l], prefill)
        seq = lambda a: a.reshape(b, t_len, a.shape[-1])
        wo = w["w_out"][l]
        wo_parts = [wo[g * GROUP_W:(g + 1) * GROUP_W] for g in range(4)]
        if prefill:
            up, k, v, qp, ka, va, cu, x4, kmean = outs
            yb = _moba_prefill(seq(qp), seq(ka), seq(va), kmean.reshape(b, t_len // MOBA_BLOCK, 1, 2 * GROUP_W))
            wo_parts[1] = jnp.pad(wo_parts[1].reshape(N_HEADS, HEAD_DIM, d),
                                  ((0, 0), (0, 128 - HEAD_DIM), (0, 0))).reshape(N_HEADS * 128, d)
        else:
            up, q, k, v, cu, x4 = outs
            yb = _moba_decode(seq(q), seq(k), seq(v), cache_k, cache_v, page_table, l)
        ya, yc, pool_new, conv_new = _poolconv(
            seq(up), seq(cu), pool_st[l], conv_st[l], q_pos0, w["pool_w"][l], w["pool_scale"][l],
            w["conv_w"][l], w["conv_b"][l], w["conv_ln_g"][l], w["conv_ln_b"][l], w["conv_pw"][l])
        yd, s_new = _hgrn(seq(x4), hgrn_st[l], lbs[l], w["hgrn_norm"][l])
        flat = lambda a: a.reshape(n, a.shape[-1])
        xf = _ffn(xf, w["ln_ffn2"][l], w["ffn2_w_gate"][l], w["ffn2_w_up"][l], w["ffn2_w_down"][l],
                  mix=tuple(zip((flat(ya), flat(yb), flat(yc), flat(yd)), wo_parts)))
        ks.append(k.reshape(b, t_len, N_HEADS, HEAD_DIM))
        vs.append(v.reshape(b, t_len, N_HEADS, HEAD_DIM))
        pools.append(pool_new)
        convs.append(conv_new)
        hgrns.append(s_new)
    return (xf.reshape(b, t_len, d), jnp.stack(ks), jnp.stack(vs), jnp.stack(pools), jnp.stack(convs),
            jnp.stack(hgrns))


def kernel(x_prompt, x_sample, cache_k, cache_v, page_table, state_pool, state_conv, state_hgrn, ln_ffn1, ffn1_w_gate, ffn1_w_up, ffn1_w_down, ln_mix, w_in, w_out, pool_w, pool_scale, q_norm, k_norm, conv_w, conv_b, conv_ln_g, conv_ln_b, conv_pw, hgrn_lower_bounds, hgrn_norm, ln_ffn2, ffn2_w_gate, ffn2_w_up, ffn2_w_down):
    w = dict(ln_ffn1=ln_ffn1, ln_mix=ln_mix, ln_ffn2=ln_ffn2, pool_w=pool_w, pool_scale=pool_scale,
             q_norm=q_norm, k_norm=k_norm, conv_w=conv_w, conv_b=conv_b, conv_ln_g=conv_ln_g,
             conv_ln_b=conv_ln_b, conv_pw=conv_pw, hgrn_lower_bounds=hgrn_lower_bounds, hgrn_norm=hgrn_norm)
    for name, arr in (("ffn1_w_gate", ffn1_w_gate), ("ffn1_w_up", ffn1_w_up), ("ffn1_w_down", ffn1_w_down),
                      ("w_in", w_in), ("w_out", w_out), ("ffn2_w_gate", ffn2_w_gate), ("ffn2_w_up", ffn2_w_up),
                      ("ffn2_w_down", ffn2_w_down)):
        w[name] = arr.astype(BF16)
    depth = ln_ffn1.shape[0]
    bp = x_prompt.shape[0]
    dt = x_prompt.dtype
    zero_pool = jnp.zeros((depth, bp) + state_pool.shape[2:], dt)
    zero_conv = jnp.zeros((depth, bp) + state_conv.shape[2:], dt)
    zero_hgrn = jnp.zeros((depth, bp) + state_hgrn.shape[2:], dt)
    y_p, k_p, v_p, pool_p, conv_p, hgrn_p = _trunk(
        x_prompt, 0, zero_pool, zero_conv, zero_hgrn, None, None, None, w)
    past_len = page_table.shape[1] * cache_k.shape[2]
    to_pages = lambda c: jnp.transpose(c, (0, 1, 3, 4, 2)).reshape(c.shape[:2] + (GROUP_W, c.shape[2]))
    ck, cv = to_pages(cache_k), to_pages(cache_v)
    y_s, k_s, v_s, pool_s, conv_s, hgrn_s = _trunk(
        x_sample, past_len, state_pool, state_conv, state_hgrn, ck, cv, page_table, w)
    return (y_p, y_s, k_p, v_p, k_s, v_s, pool_p, pool_s, conv_p, conv_s, hgrn_p, hgrn_s)
```

```python
import functools

import jax
import jax.numpy as jnp
from jax import lax
from jax.experimental import pallas as pl
from jax.experimental.pallas import tpu as pltpu

F32 = jnp.float32
BF16 = jnp.bfloat16
HI = lax.Precision.HIGHEST

NORM_EPS = 1e-6
GROUP_W = 256
N_HEADS = 4
HEAD_DIM = GROUP_W // N_HEADS
N_IN_SPLITS = 10
POOL_WINDOWS = (2, 4, 8, 16)
POOL_PAD = 16
CONV_W = 31
CONV_PAD = 32
MOBA_BLOCK = 256
MOBA_TOPK = 3
ROPE_DIM = HEAD_DIM // 4
ROPE_THETA = 500000.0
ATTN_SCALE = HEAD_DIM ** -0.5
LOG2_E = 1.4426950408889634
NEG_BIG = -1e30
HGRN_CHUNK = 64
HGRN_SUB = 8
PAGES_PER_CHUNK = 32
MXU_TILE = 256
VMEM_LIMIT = 56 * 1024 * 1024


def _params(sem=None):
    return pltpu.CompilerParams(dimension_semantics=sem, vmem_limit_bytes=VMEM_LIMIT)


def _dot(a, b):
    return jnp.dot(a, b, preferred_element_type=F32)


def _dot_nt(a, b, precision=None):
    return lax.dot_general(a, b, (((1,), (1,)), ((), ())), precision=precision,
                           preferred_element_type=F32)


def _lane_head(shape):
    return lax.broadcasted_iota(jnp.int32, shape, len(shape) - 1) // HEAD_DIM


def _head_ones():
    r = lax.broadcasted_iota(jnp.int32, (GROUP_W, GROUP_W), 0) // HEAD_DIM
    c = lax.broadcasted_iota(jnp.int32, (GROUP_W, GROUP_W), 1) // HEAD_DIM
    return r == c


def _sigmoid(x):
    return 1.0 / (1.0 + jnp.exp(-x))


def _ffn_kernel(*refs, n_mix, ff_chunk):
    x_ref = refs[0]
    mix_refs = refs[1:1 + 2 * n_mix]
    g_ref, wg_ref, wu_ref, wd_ref, o_ref = refs[1 + 2 * n_mix:]
    x = x_ref[...]
    for idx in range(n_mix):
        x = x + _dot(mix_refs[idx][...].astype(BF16), mix_refs[n_mix + idx][...])
    ms = jnp.mean(x * x, axis=-1, keepdims=True)
    h = (x * lax.rsqrt(ms + NORM_EPS) * g_ref[...]).astype(BF16)
    acc = None
    for c0, c1 in zip((0,) + ff_chunk, ff_chunk + (wg_ref.shape[1],)):
        g = _dot(h, wg_ref[:, c0:c1])
        u = _dot(h, wu_ref[:, c0:c1])
        a = (g * _sigmoid(g) * u).astype(BF16)
        part = _dot(a, wd_ref[c0:c1, :])
        acc = part if acc is None else acc + part
    o_ref[...] = x + 0.5 * acc


def _resident(shape):
    return pl.BlockSpec(shape, lambda i: (0,) * len(shape), pipeline_mode=pl.Buffered(1))


def _ffn(x, gain, wg, wu, wd, mix=()):
    n, d = x.shape
    d_ff = wg.shape[1]
    tm = 512 if n % 512 == 0 else n
    ff_chunk = (-(-(d_ff // MXU_TILE) // 2) * MXU_TILE,) if d_ff % MXU_TILE == 0 and d_ff > MXU_TILE else ()
    tok = lambda i: (i, 0)
    in_specs = [pl.BlockSpec((tm, d), tok)]
    in_specs += [pl.BlockSpec((tm, y.shape[1]), tok) for y, _ in mix]
    in_specs += [_resident(w.shape) for _, w in mix]
    in_specs += [_resident((1, d)), _resident(wg.shape), _resident(wu.shape), _resident(wd.shape)]
    args = [x] + [y for y, _ in mix] + [w for _, w in mix] + [gain.reshape(1, d), wg, wu, wd]
    return pl.pallas_call(
        functools.partial(_ffn_kernel, n_mix=len(mix), ff_chunk=ff_chunk),
        grid=(n // tm,),
        in_specs=in_specs,
        out_specs=pl.BlockSpec((tm, d), tok),
        out_shape=jax.ShapeDtypeStruct((n, d), F32),
        compiler_params=_params(("parallel",)),
        name="ffn_merge" if mix else "ffn",
    )(*args)


def _spread_heads(x, fill):
    lane = lax.broadcasted_iota(jnp.int32, (x.shape[0], 128), 1)
    odd = pltpu.roll(x, GROUP_W - HEAD_DIM, axis=1)
    parts = (x[:, :128], odd[:, :128], x[:, 128:], odd[:, 128:])
    return jnp.concatenate([jnp.where(lane < HEAD_DIM, p, fill) for p in parts], axis=1)


def _inproj_kernel(*refs, prefill, tm, per_seq):
    x_ref, g_ref, w_ref, qg_ref, kg_ref, cos_ref, sin_ref = refs[:7]
    if prefill:
        up_ref, kt_ref, vt_ref, qp_ref, ka_ref, va_ref, cu_ref, hg_ref, km_ref = refs[7:]
    else:
        up_ref, q_ref, k_ref, v_ref, cu_ref, hg_ref = refs[7:]
    x = x_ref[...]
    ms = jnp.mean(x * x, axis=-1, keepdims=True)
    h = (x * lax.rsqrt(ms + NORM_EPS) * g_ref[...]).astype(BF16)

    def proj(c):
        return _dot(h, w_ref[:, c * GROUP_W:(c + 1) * GROUP_W])

    ones = _head_ones().astype(F32)
    cos = cos_ref[...]
    sin = sin_ref[...]
    d = lax.broadcasted_iota(jnp.int32, cos.shape, 1) % HEAD_DIM
    half = ROPE_DIM // 2

    def norm_rope(p, gain):
        ms_h = jnp.dot(p * p, ones, precision=HI, preferred_element_type=F32) * (1.0 / HEAD_DIM)
        y = p * lax.rsqrt(ms_h + NORM_EPS) * gain
        partner = jnp.where(d < half, pltpu.roll(y, GROUP_W - half, axis=1), pltpu.roll(y, half, axis=1))
        return y * cos + partner * sin

    up_ref[...] = proj(0)
    q = norm_rope(proj(1), qg_ref[...])
    k = norm_rope(proj(2), kg_ref[...])
    v = proj(3)
    cu_ref[...] = proj(4) * _sigmoid(proj(5))
    for c in range(4):
        hg_ref[:, c * GROUP_W:(c + 1) * GROUP_W] = proj(6 + c)
    if not prefill:
        q_ref[...] = q
        k_ref[...] = k
        v_ref[...] = v
        return
    kt_ref[0] = k.T
    vt_ref[0] = v.T
    lane = lax.broadcasted_iota(jnp.int32, (tm, 128), 1)
    row = lax.broadcasted_iota(jnp.int32, (tm, 128), 0)
    blk = ((pl.program_id(0) % per_seq) * tm + row) // MOBA_BLOCK
    zero = jnp.zeros((tm, 128), F32)
    kp = _spread_heads(k, zero)
    qp_ref[...] = _spread_heads(q, zero)
    ka_ref[...] = _spread_heads(k, jnp.where(lane - HEAD_DIM == blk, 1.0, 0.0)).astype(BF16)
    va_ref[...] = _spread_heads(v, jnp.where(lane == HEAD_DIM, 1.0, 0.0)).astype(BF16)
    for n in range(km_ref.shape[0]):
        km_ref[n] = jnp.mean(kp[n * MOBA_BLOCK:(n + 1) * MOBA_BLOCK, :], axis=0, keepdims=True)


def _rope_tables(t_len, q_pos0):
    half = ROPE_DIM // 2
    inv = ROPE_THETA ** (-jnp.arange(half, dtype=F32) * 2.0 / ROPE_DIM)
    pos = q_pos0 + jnp.arange(t_len, dtype=jnp.int32)
    ang = pos.astype(F32)[:, None] * inv
    c, s = jnp.cos(ang), jnp.sin(ang)
    rest = HEAD_DIM - ROPE_DIM
    cos_h = jnp.concatenate([c, c, jnp.ones((t_len, rest), F32)], axis=1)
    sin_h = jnp.concatenate([-s, s, jnp.zeros((t_len, rest), F32)], axis=1)
    return jnp.tile(cos_h, (1, N_HEADS)), jnp.tile(sin_h, (1, N_HEADS))


def _inproj(x, t_len, q_pos0, gain, w_in, q_gain, k_gain, prefill):
    n, d = x.shape
    if t_len % 512 == 0:
        tm = 512
    else:
        tm = t_len * max(r for r in range(1, n // t_len + 1) if (n // t_len) % r == 0 and t_len * r <= 512)
    assert not prefill or (tm % MOBA_BLOCK == 0 and t_len // MOBA_BLOCK <= 128 - HEAD_DIM)
    cos_t, sin_t = _rope_tables(t_len, q_pos0)
    if tm > t_len:
        cos_t, sin_t = jnp.tile(cos_t, (tm // t_len, 1)), jnp.tile(sin_t, (tm // t_len, 1))
    per_seq = max(t_len // tm, 1)
    tok = lambda i: (i, 0)
    tab = lambda i: (i % per_seq, 0)
    wide = 2 * GROUP_W
    grp = jax.ShapeDtypeStruct((n, GROUP_W), F32)
    gspec = pl.BlockSpec((tm, GROUP_W), tok)
    wspec = pl.BlockSpec((tm, wide), tok)
    x4 = jax.ShapeDtypeStruct((n, 4 * GROUP_W), F32)
    x4spec = pl.BlockSpec((tm, 4 * GROUP_W), tok)
    if prefill:
        kv_t = jax.ShapeDtypeStruct((n // t_len, GROUP_W, t_len), F32)
        kv_tspec = pl.BlockSpec((1, GROUP_W, tm), lambda i: (i // per_seq, 0, i % per_seq))
        out_shape = [grp, kv_t, kv_t, jax.ShapeDtypeStruct((n, wide), F32), jax.ShapeDtypeStruct((n, wide), BF16),
                     jax.ShapeDtypeStruct((n, wide), BF16), grp, x4,
                     jax.ShapeDtypeStruct((n // MOBA_BLOCK, 1, wide), F32)]
        out_specs = [gspec, kv_tspec, kv_tspec, wspec, wspec, wspec, gspec, x4spec,
                     pl.BlockSpec((tm // MOBA_BLOCK, 1, wide), lambda i: (i, 0, 0))]
    else:
        out_shape = [grp, grp, grp, grp, grp, x4]
        out_specs = [gspec] * 5 + [x4spec]
    return pl.pallas_call(
        functools.partial(_inproj_kernel, prefill=prefill, tm=tm, per_seq=per_seq),
        grid=(n // tm,),
        in_specs=[pl.BlockSpec((tm, d), tok), _resident((1, d)), _resident(w_in.shape),
                  _resident((1, GROUP_W)), _resident((1, GROUP_W)),
                  pl.BlockSpec((tm, GROUP_W), tab), pl.BlockSpec((tm, GROUP_W), tab)],
        out_specs=out_specs,
        out_shape=out_shape,
        compiler_params=_params(("parallel",)),
        name="inproj",
    )(x, gain.reshape(1, d), w_in, jnp.tile(q_gain, N_HEADS).reshape(1, GROUP_W),
      jnp.tile(k_gain, N_HEADS).reshape(1, GROUP_W), cos_t, sin_t)


def _poolconv_kernel(up_ref, cu_ref, pprev_ref, cprev_ref, pw_ref, ps_ref, cw_ref, cb_ref, lg_ref, lb_ref, cpw_ref,
                     ya_ref, yc_ref, pnew_ref, cnew_ref, pbuf, cbuf, *, tt, q_pos0, rows):
    t = pl.program_id(1)

    @pl.when(t == 0)
    def _():
        pbuf[0:POOL_PAD, :] = pprev_ref[0]
        cbuf[0:CONV_PAD, :] = cprev_ref[0]

    pbuf[POOL_PAD:POOL_PAD + tt, :] = up_ref[0]
    cbuf[CONV_PAD:CONV_PAD + tt, :] = cu_ref[0]

    for r0 in range(0, tt, rows):
        pos1 = (q_pos0 + t * tt + r0 + 1 + lax.broadcasted_iota(jnp.int32, (rows, 128), 0)).astype(F32)
        first = lax.broadcasted_iota(jnp.int32, (rows, 128), 1) < (GROUP_W // 4)
        pooled = []
        for half_i, (w_a, w_b) in enumerate(((POOL_WINDOWS[0], POOL_WINDOWS[1]), (POOL_WINDOWS[2], POOL_WINDOWS[3]))):
            lanes = slice(half_i * 128, (half_i + 1) * 128)
            u = pbuf[POOL_PAD + r0:POOL_PAD + r0 + rows, lanes]
            acc = u
            s_a = None
            for j in range(1, w_b):
                acc = acc + pbuf[POOL_PAD + r0 - j:POOL_PAD + r0 - j + rows, lanes]
                if j == w_a - 1:
                    s_a = acc
            avg = jnp.where(first, s_a / jnp.minimum(float(w_a), pos1), acc / jnp.minimum(float(w_b), pos1))
            pooled.append((avg - u).astype(BF16))
        pooled = jnp.concatenate(pooled, axis=1)
        ya_ref[0, r0:r0 + rows, :] = _dot(pooled, pw_ref[...]) * ps_ref[...]

        acc = cbuf[CONV_PAD - (CONV_W - 1) + r0:CONV_PAD - (CONV_W - 1) + r0 + rows, :] * cw_ref[0:1, :]
        for j in range(1, CONV_W):
            base = CONV_PAD - (CONV_W - 1) + j + r0
            acc = acc + cbuf[base:base + rows, :] * cw_ref[j:j + 1, :]
        y = acc + cb_ref[...]
        mu = jnp.mean(y, axis=-1, keepdims=True)
        var = jnp.mean(jnp.square(y - mu), axis=-1, keepdims=True)
        y = (y - mu) * lax.rsqrt(var + NORM_EPS) * lg_ref[...] + lb_ref[...]
        y = y * _sigmoid(y)
        yc_ref[0, r0:r0 + rows, :] = _dot(y.astype(BF16), cpw_ref[...])

    ptail = pbuf[tt:tt + POOL_PAD, :]
    ctail = cbuf[tt:tt + CONV_PAD, :]
    pnew_ref[0] = ptail
    cnew_ref[0] = ctail
    pbuf[0:POOL_PAD, :] = ptail
    cbuf[0:CONV_PAD, :] = ctail


def _poolconv(up, cu, pool_prev, conv_prev, q_pos0, pool_w, pool_scale, conv_w, conv_b, ln_g, ln_b, conv_pw):
    b, t_len, _ = up.shape
    tt = 512 if t_len % 512 == 0 else t_len
    rows = 64 if tt % 64 == 0 else tt
    pprev = jnp.pad(pool_prev, ((0, 0), (POOL_PAD - pool_prev.shape[1], 0), (0, 0)))
    cprev = jnp.pad(conv_prev, ((0, 0), (CONV_PAD - conv_prev.shape[1], 0), (0, 0)))
    pw_bd = jax.scipy.linalg.block_diag(*[pool_w[g] for g in range(pool_w.shape[0])]).astype(BF16)
    cw_pad = jnp.pad(conv_w, ((0, CONV_PAD - CONV_W), (0, 0)))
    seq = lambda i, j: (i, j, 0)
    per_b = lambda i, j: (i, 0, 0)
    const = lambda i, j: (0, 0)
    row = pl.BlockSpec((1, GROUP_W), const)
    ya, yc, pnew, cnew = pl.pallas_call(
        functools.partial(_poolconv_kernel, tt=tt, q_pos0=q_pos0, rows=rows),
        grid=(b, t_len // tt),
        in_specs=[pl.BlockSpec((1, tt, GROUP_W), seq), pl.BlockSpec((1, tt, GROUP_W), seq),
                  pl.BlockSpec((1, POOL_PAD, GROUP_W), per_b), pl.BlockSpec((1, CONV_PAD, GROUP_W), per_b),
                  pl.BlockSpec((GROUP_W, GROUP_W), const), row,
                  pl.BlockSpec((CONV_PAD, GROUP_W), const), row, row, row,
                  pl.BlockSpec((GROUP_W, GROUP_W), const)],
        out_specs=[pl.BlockSpec((1, tt, GROUP_W), seq), pl.BlockSpec((1, tt, GROUP_W), seq),
                   pl.BlockSpec((1, POOL_PAD, GROUP_W), per_b), pl.BlockSpec((1, CONV_PAD, GROUP_W), per_b)],
        out_shape=[jax.ShapeDtypeStruct((b, t_len, GROUP_W), F32), jax.ShapeDtypeStruct((b, t_len, GROUP_W), F32),
                   jax.ShapeDtypeStruct((b, POOL_PAD, GROUP_W), F32), jax.ShapeDtypeStruct((b, CONV_PAD, GROUP_W), F32)],
        scratch_shapes=[pltpu.VMEM((POOL_PAD + tt, GROUP_W), F32), pltpu.VMEM((CONV_PAD + tt, GROUP_W), F32)],
        compiler_params=_params(("parallel", "arbitrary")),
        name="poolconv",
    )(up, cu, pprev, cprev, pw_bd, pool_scale.reshape(1, GROUP_W), cw_pad, conv_b.reshape(1, GROUP_W),
      ln_g.reshape(1, GROUP_W), ln_b.reshape(1, GROUP_W), conv_pw.astype(BF16))
    return ya, yc, pnew[:, POOL_PAD - pool_prev.shape[1]:], cnew[:, CONV_PAD - conv_prev.shape[1]:]


def _stack_heads(q):
    head = _lane_head(q.shape)
    return jnp.concatenate([jnp.where(head == h, q, 0.0) for h in range(N_HEADS)], axis=0)


def _select_blocks(gate, n_past):
    lane = lax.broadcasted_iota(jnp.int32, gate.shape, 1)
    past = lane < n_past
    g = jnp.where(past, gate, NEG_BIG)
    sel = jnp.zeros(gate.shape, jnp.bool_)
    for _ in range(MOBA_TOPK):
        mx = jnp.max(g, axis=1, keepdims=True)
        idx = jnp.min(jnp.where(g == mx, lane, 1 << 20), axis=1, keepdims=True)
        pick = lane == idx
        sel = jnp.logical_or(sel, pick)
        g = jnp.where(pick, -jnp.inf, g)
    return jnp.logical_and(sel, past)


def _select_block_rows(gate_t, n_past, nblk):
    n = lax.broadcasted_iota(jnp.int32, gate_t.shape, 0)
    past = n < n_past
    g = jnp.where(past, gate_t, jnp.where(n < nblk, NEG_BIG, -jnp.inf))
    sel = jnp.zeros(gate_t.shape, jnp.bool_)
    for _ in range(MOBA_TOPK):
        mx = jnp.max(g, axis=0, keepdims=True)
        idx = jnp.min(jnp.where(g == mx, n, 1 << 20), axis=0, keepdims=True)
        pick = n == idx
        sel = jnp.logical_or(sel, pick)
        g = jnp.where(pick, -jnp.inf, g)
    return jnp.logical_and(sel, past)


def _moba_prefill_kernel(qp_ref, ka_ref, va_ref, km_ref, o_ref,
                         kmr_ref, qa_ref, s_ref, mx_ref, m_ref, acc_ref, *, nblk):
    i = pl.program_id(1)
    blk = MOBA_BLOCK
    kb = kmr_ref.shape[1]
    heads = [slice(h * 128, (h + 1) * 128) for h in range(N_HEADS)]

    @pl.when(i == 0)
    def _():
        kmr_ref[...] = jnp.zeros_like(kmr_ref)
        for h in range(N_HEADS):
            for n in range(nblk):
                kmr_ref[h, n:n + 1, :] = km_ref[0, n][:, heads[h]]

    n_row = lax.broadcasted_iota(jnp.int32, (kb, blk), 0)
    for h in range(N_HEADS):
        qh = qp_ref[0, :, heads[h]]
        gate_t = _dot_nt(kmr_ref[h], qh, precision=HI)
        sel = _select_block_rows(gate_t, i, nblk)
        bias_t = jnp.where(jnp.logical_or(sel, n_row == i), 0.0, NEG_BIG)
        bias_t = jnp.concatenate([jnp.zeros((HEAD_DIM, blk), F32), bias_t,
                                  jnp.zeros((128 - HEAD_DIM - kb, blk), F32)], axis=0)
        qa_ref[h] = (qh * (ATTN_SCALE * LOG2_E) + bias_t.T).astype(BF16)
        mx_ref[h] = jnp.full((blk, 128), NEG_BIG, F32)
        acc_ref[h] = jnp.zeros((blk, 128), F32)

    row = lax.broadcasted_iota(jnp.int32, (blk, blk), 0)
    col = lax.broadcasted_iota(jnp.int32, (blk, blk), 1)
    n_pairs = (i + 2) // 2

    def scores(jj, carry):
        for u in range(2):
            j = 2 * jj + u
            j0 = pl.multiple_of(j * blk, blk)
            allow = col + (j - i) * blk <= row
            for h in range(N_HEADS):
                s = jnp.where(allow, _dot_nt(qa_ref[h], ka_ref[0, pl.ds(j0, blk), heads[h]]), NEG_BIG)
                s_ref[h, j] = s
                mx_ref[h] = jnp.maximum(mx_ref[h], jnp.maximum(s[:, :128], s[:, 128:]))
        return carry

    lax.fori_loop(0, n_pairs, scores, 0)
    for h in range(N_HEADS):
        m_ref[h] = jnp.broadcast_to(jnp.max(mx_ref[h], axis=1, keepdims=True), (blk, 128))

    def weighted(jj, carry):
        for u in range(2):
            j = 2 * jj + u
            j0 = pl.multiple_of(j * blk, blk)
            for h in range(N_HEADS):
                s = s_ref[h, j]
                m = m_ref[h]
                p = jnp.concatenate([jnp.exp2(s[:, :128] - m), jnp.exp2(s[:, 128:] - m)], axis=1).astype(BF16)
                acc_ref[h] += _dot(p, va_ref[0, pl.ds(j0, blk), heads[h]])
        return carry

    lax.fori_loop(0, n_pairs, weighted, 0)
    lane = lax.broadcasted_iota(jnp.int32, (blk, 128), 1)
    for h in range(N_HEADS):
        acc = acc_ref[h]
        o_ref[0, :, heads[h]] = jnp.where(lane < HEAD_DIM, acc / acc[:, HEAD_DIM:HEAD_DIM + 1], 0.0)


def _moba_prefill(qp, ka, va, kmean):
    b, t_len, wide = qp.shape
    nblk = t_len // MOBA_BLOCK
    assert nblk % 2 == 0 and nblk <= 128 - HEAD_DIM
    kb = -(-nblk // 8) * 8
    blk = MOBA_BLOCK
    per_b = lambda i, j: (i, 0, 0)
    return pl.pallas_call(
        functools.partial(_moba_prefill_kernel, nblk=nblk),
        grid=(b, nblk),
        in_specs=[pl.BlockSpec((1, blk, wide), lambda i, j: (i, j, 0)),
                  pl.BlockSpec((1, t_len, wide), per_b, pipeline_mode=pl.Buffered(1)),
                  pl.BlockSpec((1, t_len, wide), per_b, pipeline_mode=pl.Buffered(1)),
                  pl.BlockSpec((1, nblk, 1, wide), lambda i, j: (i, 0, 0, 0))],
        out_specs=pl.BlockSpec((1, blk, wide), lambda i, j: (i, j, 0)),
        out_shape=jax.ShapeDtypeStruct((b, t_len, wide), F32),
        scratch_shapes=[pltpu.VMEM((N_HEADS, kb, 128), F32), pltpu.VMEM((N_HEADS, blk, 128), BF16),
                        pltpu.VMEM((N_HEADS, nblk, blk, blk), F32), pltpu.VMEM((N_HEADS, blk, 128), F32),
                        pltpu.VMEM((N_HEADS, blk, 128), F32), pltpu.VMEM((N_HEADS, blk, 128), F32)],
        compiler_params=_params(("parallel", "arbitrary")),
        name="moba_prefill",
    )(qp, ka, va, kmean)


def _moba_decode_kernel(pt_ref, q_ref, kn_ref, vn_ref, ck_ref, cv_ref, o_ref,
                        buf, sem, s_ref, p_ref, own_ref, *, layer, n_pages, tq):
    b = pl.program_id(0)
    ch = PAGES_PER_CHUNK
    n_chunks = n_pages // ch
    page = buf.shape[3]
    pages_per_blk = MOBA_BLOCK // page
    n_past = n_pages // pages_per_blk
    rows = N_HEADS * tq

    def page_copy(cache_ref, c, pg, slot):
        return pltpu.make_async_copy(cache_ref.at[layer, pt_ref[b, c * ch + pg]], buf.at[slot, pg], sem.at[slot])

    def start_chunk(cache_ref, c, slot):
        for pg in range(ch):
            page_copy(cache_ref, c, pg, slot).start()

    def wait_chunk(cache_ref, c, slot):
        for pg in range(ch):
            page_copy(cache_ref, c, pg, slot).wait()

    q4 = _stack_heads(q_ref[0]) * ATTN_SCALE
    q_hi = q4.astype(BF16)
    qq = jnp.concatenate([q_hi, (q4 - q_hi.astype(F32)).astype(BF16)], axis=0)

    def both(s2):
        return s2[:rows] + s2[rows:]

    start_chunk(ck_ref, 0, 0)

    def k_chunk(c, carry):
        slot = c % 2

        @pl.when(c + 1 < n_chunks)
        def _():
            start_chunk(ck_ref, c + 1, 1 - slot)

        @pl.when(c + 1 == n_chunks)
        def _():
            start_chunk(cv_ref, 0, 1 - slot)

        wait_chunk(ck_ref, c, slot)
        for pg in range(ch):
            s_ref[c * ch + pg] = both(_dot(qq, buf[slot, pg].astype(BF16)))
        return carry

    lax.fori_loop(0, n_chunks, k_chunk, 0)

    lane = lax.broadcasted_iota(jnp.int32, (rows, 128), 1)
    gate = jnp.zeros((rows, 128), F32)
    for n in range(n_past):
        tot = s_ref[n * pages_per_blk]
        for pg in range(1, pages_per_blk):
            tot = tot + s_ref[n * pages_per_blk + pg]
        gate = jnp.where(lane == n, jnp.sum(tot, axis=1, keepdims=True), gate)
    gate = gate * (1.0 / (MOBA_BLOCK * ATTN_SCALE))
    sel = _select_blocks(gate, n_past).astype(F32)

    own_ref[...] = jnp.zeros_like(own_ref)
    own_ref[0, 0:tq, :] = kn_ref[0]
    own_ref[1, 0:tq, :] = vn_ref[0]
    s_own = both(_dot_nt(qq, own_ref[0].astype(BF16)))
    s_own = jnp.where(lane <= lax.broadcasted_iota(jnp.int32, (rows, 128), 0) % tq, s_own, NEG_BIG)

    mx = s_own
    for n in range(n_past):
        ok = sel[:, n:n + 1] > 0.5
        for pg in range(n * pages_per_blk, (n + 1) * pages_per_blk):
            s_pg = jnp.where(ok, s_ref[pg], NEG_BIG)
            s_ref[pg] = s_pg
            mx = jnp.maximum(mx, s_pg)
    m = jnp.max(mx, axis=1, keepdims=True)
    p_own = jnp.exp(s_own - m)
    lsum = p_own
    for pg in range(n_pages):
        p = jnp.exp(s_ref[pg] - m)
        lsum = lsum + p
        p_ref[pg] = p.astype(BF16)
    l = jnp.sum(lsum, axis=1, keepdims=True)

    def v_chunk(c, acc):
        slot = (n_chunks + c) % 2

        @pl.when(c + 1 < n_chunks)
        def _():
            start_chunk(cv_ref, c + 1, 1 - slot)

        wait_chunk(cv_ref, c, slot)
        for pg in range(ch):
            acc = acc + _dot_nt(p_ref[c * ch + pg], buf[slot, pg].astype(BF16))
        return acc

    acc = lax.fori_loop(0, n_chunks, v_chunk, _dot(p_own.astype(BF16), own_ref[1].astype(BF16)))
    acc = acc / l
    head = _lane_head((tq, GROUP_W))
    out = jnp.zeros((tq, GROUP_W), F32)
    for h in range(N_HEADS):
        out = out + jnp.where(head == h, acc[h * tq:(h + 1) * tq, :], 0.0)
    o_ref[0] = out


def _moba_decode(q, k_new, v_new, cache_k, cache_v, page_table, layer):
    b, tq, _ = q.shape
    n_pages = page_table.shape[1]
    page = cache_k.shape[3]
    assert page == 128 and MOBA_BLOCK % page == 0 and n_pages % (MOBA_BLOCK // page) == 0
    assert n_pages % PAGES_PER_CHUNK == 0 and tq <= 128 and n_pages * page // MOBA_BLOCK <= 128
    rows = N_HEADS * tq
    per_b = lambda i, pt: (i, 0, 0)
    grid_spec = pltpu.PrefetchScalarGridSpec(
        num_scalar_prefetch=1,
        grid=(b,),
        in_specs=[pl.BlockSpec((1, tq, GROUP_W), per_b), pl.BlockSpec((1, tq, GROUP_W), per_b),
                  pl.BlockSpec((1, tq, GROUP_W), per_b),
                  pl.BlockSpec(memory_space=pl.ANY), pl.BlockSpec(memory_space=pl.ANY)],
        out_specs=pl.BlockSpec((1, tq, GROUP_W), per_b),
        scratch_shapes=[pltpu.VMEM((2, PAGES_PER_CHUNK, GROUP_W, page), F32),
                        pltpu.SemaphoreType.DMA((2,)),
                        pltpu.VMEM((n_pages, rows, page), F32), pltpu.VMEM((n_pages, rows, page), BF16),
                        pltpu.VMEM((2, 128, GROUP_W), F32)],
    )
    return pl.pallas_call(
        functools.partial(_moba_decode_kernel, layer=layer, n_pages=n_pages, tq=tq),
        grid_spec=grid_spec,
        out_shape=jax.ShapeDtypeStruct((b, tq, GROUP_W), F32),
        compiler_params=_params(("arbitrary",)),
        name="moba_decode",
    )(page_table, q, k_new, v_new, cache_k, cache_v)


def _hgrn_kernel(x_ref, st0_ref, lb_ref, ng_ref, y_ref, stout_ref, st_ref, qst_ref, upd_ref, dec_ref,
                 *, tt, chunk, sub):
    t = pl.program_id(1)
    n_sub = chunk // sub

    @pl.when(t == 0)
    def _():
        st_ref[...] = st0_ref[0]

    lb = lb_ref[...]
    head = _lane_head((1, GROUP_W))
    hmask = [head == h for h in range(N_HEADS)]
    bd = _head_ones()
    ones_b = bd.astype(BF16)
    ones_f = bd.astype(F32)
    tri = (lax.broadcasted_iota(jnp.int32, (chunk, chunk), 1)
           <= lax.broadcasted_iota(jnp.int32, (chunk, chunk), 0)).astype(F32)
    t_ge = [lax.broadcasted_iota(jnp.int32, (sub, GROUP_W), 0) >= s for s in range(sub)]

    n_chunks = tt // chunk
    if n_sub > 1:
        n_used = sub * (n_sub * (n_sub - 1) // 2)
        n_col = -(-n_used // 128) * 128
        col =lax.broadcasted_iota(jnp.int32, (N_HEADS * chunk, n_col), 1)
        row_sub = (lax.broadcasted_iota(jnp.int32, (N_HEADS * chunk, n_col), 0) % chunk) // sub
        keep = jnp.zeros((N_HEADS * chunk, n_col), jnp.bool_)
        c0 = 0
        for i in range(1, n_sub):
            keep = jnp.logical_or(keep, jnp.logical_and(row_sub == i, jnp.logical_and(col >= c0, col < c0 + i * sub)))
            c0 += i * sub

    def local_part(ci):
        r0 = ci * chunk if isinstance(ci, int) else pl.multiple_of(ci * chunk, chunk)
        hq = x_ref[0, pl.ds(r0, chunk), 0 * GROUP_W:1 * GROUP_W]
        hf = x_ref[0, pl.ds(r0, chunk), 1 * GROUP_W:2 * GROUP_W]
        hi = x_ref[0, pl.ds(r0, chunk), 2 * GROUP_W:3 * GROUP_W]
        f = lb + (1.0 - lb) * _sigmoid(hf)
        kin = 1.0 - f
        bc = jnp.dot(tri, jnp.log2(f), precision=HI, preferred_element_type=F32)
        b_last = bc[chunk - 1:chunk, :]
        qst_ref[pl.ds(r0, chunk), :] = hq * jnp.exp2(bc)
        k_state = (kin * jnp.exp2(b_last - bc)).astype(BF16)
        upd_ref[ci] = jnp.where(bd, _dot(hi.T.astype(BF16), k_state), 0.0)
        dec_ref[ci] = jnp.exp2(b_last)

        o = None
        if n_sub > 1:
            bq = jnp.concatenate(
                [jnp.zeros((sub, GROUP_W), F32)]
                + [jnp.broadcast_to(bc[i * sub - 1:i * sub, :], (sub, GROUP_W)) for i in range(1, n_sub)], axis=0)
            q_off = hq * jnp.exp2(bc - bq)
            q4 = jnp.concatenate([jnp.where(hmask[h], q_off, 0.0) for h in range(N_HEADS)], axis=0).astype(BF16)
            pad = [jnp.zeros((n_col - n_used, GROUP_W), F32)] if n_col > n_used else []
            k_st = jnp.concatenate(
                [kin[0:i * sub, :] * jnp.exp2(bc[i * sub - 1:i * sub, :] - bc[0:i * sub, :])
                 for i in range(1, n_sub)] + pad, axis=0).astype(BF16)
            v_st = jnp.concatenate([hi[0:i * sub, :] for i in range(1, n_sub)] + pad, axis=0).astype(BF16)
            att = jnp.where(keep, _dot_nt(q4, k_st), 0.0).astype(BF16)
            res = _dot(att, v_st)
            o = jnp.where(hmask[0], res[0:chunk], 0.0)
            for h in range(1, N_HEADS):
                o = o + jnp.where(hmask[h], res[h * chunk:(h + 1) * chunk], 0.0)

        d_rows = []
        for i in range(n_sub):
            rs = slice(i * sub, (i + 1) * sub)
            q_i, bc_i, k_i = hq[rs], bc[rs], kin[rs]
            for s in range(sub):
                e = jnp.exp2(jnp.where(t_ge[s], bc_i - bc_i[s:s + 1, :], NEG_BIG))
                d_rows.append(q_i * e * k_i[s:s + 1, :])
        a = _dot(jnp.concatenate(d_rows, axis=0).astype(BF16), ones_b)
        o_diag = []
        for i in range(n_sub):
            v_i = hi[i * sub:(i + 1) * sub]
            acc = a[i * sub * sub:i * sub * sub + sub, :] * v_i[0:1, :]
            for s in range(1, sub):
                base = (i * sub + s) * sub
                acc = acc + a[base:base + sub, :] * v_i[s:s + 1, :]
            o_diag.append(acc)
        o_diag = jnp.concatenate(o_diag, axis=0) if n_sub > 1 else o_diag[0]
        y_ref[0, pl.ds(r0, chunk), :] = o_diag if o is None else o + o_diag

    def carried_part(ci, carry):
        r0 = pl.multiple_of(ci * chunk, chunk)
        st = st_ref[...]
        y_ref[0, pl.ds(r0, chunk), :] += _dot_nt(qst_ref[pl.ds(r0, chunk), :].astype(BF16), st.astype(BF16))
        st_ref[...] = st * dec_ref[ci] + upd_ref[ci]
        return carry

    if n_chunks % 2 == 0:
        def pair(cp, carry):
            local_part(2 * cp)
            local_part(2 * cp + 1)
            return carry
        lax.fori_loop(0, n_chunks // 2, pair, 0)
    else:
        for ci in range(n_chunks):
            local_part(ci)
    lax.fori_loop(0, n_chunks, carried_part, 0)

    o = y_ref[0]
    hg = x_ref[0, :, 3 * GROUP_W:4 * GROUP_W]
    ms = jnp.dot(o * o, ones_f, precision=HI, preferred_element_type=F32) * (1.0 / HEAD_DIM)
    y_ref[0] = o * lax.rsqrt(ms + NORM_EPS) * ng_ref[...] * (hg * _sigmoid(hg))

    @pl.when(t == pl.num_programs(1) - 1)
    def _():
        stout_ref[0] = st_ref[...]


def _hgrn(x4, state, lb, norm_g):
    b, t_len, _ = x4.shape
    chunk = HGRN_CHUNK if t_len % HGRN_CHUNK == 0 else t_len
    sub = HGRN_SUB if chunk % HGRN_SUB == 0 else chunk
    tt = 512 if t_len % 512 == 0 else (chunk if t_len == chunk else t_len)
    eye = jnp.eye(N_HEADS, dtype=state.dtype)
    st0 = jnp.einsum('bhkv,hg->bhvgk', state, eye).reshape(b, GROUP_W, GROUP_W)
    y, st = pl.pallas_call(
        functools.partial(_hgrn_kernel, tt=tt, chunk=chunk, sub=sub),
        grid=(b, t_len // tt),
        in_specs=[pl.BlockSpec((1, tt, 4 * GROUP_W), lambda i, j: (i, j, 0)),
                  pl.BlockSpec((1, GROUP_W, GROUP_W), lambda i, j: (i, 0, 0)),
                  pl.BlockSpec((1, GROUP_W), lambda i, j: (0, 0)),
                  pl.BlockSpec((1, GROUP_W), lambda i, j: (0, 0))],
        out_specs=[pl.BlockSpec((1, tt, GROUP_W), lambda i, j: (i, j, 0)),
                   pl.BlockSpec((1, GROUP_W, GROUP_W), lambda i, j: (i, 0, 0))],
        out_shape=[jax.ShapeDtypeStruct((b, t_len, GROUP_W), F32),
                   jax.ShapeDtypeStruct((b, GROUP_W, GROUP_W), F32)],
        scratch_shapes=[pltpu.VMEM((GROUP_W, GROUP_W), F32), pltpu.VMEM((tt, GROUP_W), F32),
                        pltpu.VMEM((tt // chunk, GROUP_W, GROUP_W), F32), pltpu.VMEM((tt // chunk, 1, GROUP_W), F32)],
        compiler_params=_params(("parallel", "arbitrary")),
        name="hgrn",
    )(x4, st0, lb.reshape(1, GROUP_W), jnp.tile(norm_g, N_HEADS).reshape(1, GROUP_W))
    st5 = st.reshape(b, N_HEADS, HEAD_DIM, N_HEADS, HEAD_DIM)
    new_state = jnp.stack([st5[:, h, :, h, :] for h in range(N_HEADS)], axis=1)
    return y, jnp.swapaxes(new_state, 2, 3)


def _trunk(x, q_pos0, pool_st, conv_st, hgrn_st, cache_k, cache_v, page_table, w):
    b, t_len, d = x.shape
    n = b * t_len
    depth = w["ln_ffn1"].shape[0]
    lbs = jax.nn.softmax(w["hgrn_lower_bounds"].astype(F32), axis=0)
    lbs = jnp.cumsum(lbs, axis=0) - lbs[0]
    prefill = cache_k is None
    xf = x.reshape(n, d)
    ks, vs, pools, convs, hgrns = [], [], [], [], []
    for l in range(depth):
        xf = _ffn(xf, w["ln_ffn1"][l], w["ffn1_w_gate"][l], w["ffn1_w_up"][l], w["ffn1_w_down"][l])
        outs = _inproj(xf, t_len, q_pos0, w["ln_mix"][l], w["w_in"][l], w["q_norm"][l], w["k_norm"][l], prefill)
        seq = lambda a: a.reshape(b, t_len, a.shape[-1])
        wo = w["w_out"][l]
        wo_parts = [wo[g * GROUP_W:(g + 1) * GROUP_W] for g in range(4)]
        if prefill:
            up, kt, vt, qp, ka, va, cu, x4, kmean = outs
            k, v = (jnp.transpose(a.reshape(b, N_HEADS, HEAD_DIM, t_len), (0, 3, 1, 2)) for a in (kt, vt))
            yb = _moba_prefill(seq(qp), seq(ka), seq(va), kmean.reshape(b, t_len // MOBA_BLOCK, 1, 2 * GROUP_W))
            wo_parts[1] = jnp.pad(wo_parts[1].reshape(N_HEADS, HEAD_DIM, d),
                                  ((0, 0), (0, 128 - HEAD_DIM), (0, 0))).reshape(N_HEADS * 128, d)
        else:
            up, q, k, v, cu, x4 = outs
            yb = _moba_decode(seq(q), seq(k), seq(v), cache_k, cache_v, page_table, l)
            k, v = (a.reshape(b, t_len, N_HEADS, HEAD_DIM) for a in (k, v))
        ya, yc, pool_new, conv_new = _poolconv(
            seq(up), seq(cu), pool_st[l], conv_st[l], q_pos0, w["pool_w"][l], w["pool_scale"][l],
            w["conv_w"][l], w["conv_b"][l], w["conv_ln_g"][l], w["conv_ln_b"][l], w["conv_pw"][l])
        yd, s_new = _hgrn(seq(x4), hgrn_st[l], lbs[l], w["hgrn_norm"][l])
        flat = lambda a: a.reshape(n, a.shape[-1])
        xf = _ffn(xf, w["ln_ffn2"][l], w["ffn2_w_gate"][l], w["ffn2_w_up"][l], w["ffn2_w_down"][l],
                  mix=tuple(zip((flat(ya), flat(yb), flat(yc), flat(yd)), wo_parts)))
        ks.append(k)
        vs.append(v)
        pools.append(pool_new)
        convs.append(conv_new)
        hgrns.append(s_new)
    return (xf.reshape(b, t_len, d), jnp.stack(ks), jnp.stack(vs), jnp.stack(pools), jnp.stack(convs),
            jnp.stack(hgrns))


def kernel(x_prompt, x_sample, cache_k, cache_v, page_table, state_pool, state_conv, state_hgrn, ln_ffn1, ffn1_w_gate, ffn1_w_up, ffn1_w_down, ln_mix, w_in, w_out, pool_w, pool_scale, q_norm, k_norm, conv_w, conv_b, conv_ln_g, conv_ln_b, conv_pw, hgrn_lower_bounds, hgrn_norm, ln_ffn2, ffn2_w_gate, ffn2_w_up, ffn2_w_down):
    w = dict(ln_ffn1=ln_ffn1, ln_mix=ln_mix, ln_ffn2=ln_ffn2, pool_w=pool_w, pool_scale=pool_scale,
             q_norm=q_norm, k_norm=k_norm, conv_w=conv_w, conv_b=conv_b, conv_ln_g=conv_ln_g,
             conv_ln_b=conv_ln_b, conv_pw=conv_pw, hgrn_lower_bounds=hgrn_lower_bounds, hgrn_norm=hgrn_norm)
    for name, arr in (("ffn1_w_gate", ffn1_w_gate), ("ffn1_w_up", ffn1_w_up), ("ffn1_w_down", ffn1_w_down),
                      ("w_in", w_in), ("w_out", w_out), ("ffn2_w_gate", ffn2_w_gate), ("ffn2_w_up", ffn2_w_up),
                      ("ffn2_w_down", ffn2_w_down)):
        w[name] = arr.astype(BF16)
    depth = ln_ffn1.shape[0]
    bp = x_prompt.shape[0]
    dt = x_prompt.dtype
    zero_pool = jnp.zeros((depth, bp) + state_pool.shape[2:], dt)
    zero_conv = jnp.zeros((depth, bp) + state_conv.shape[2:], dt)
    zero_hgrn = jnp.zeros((depth, bp) + state_hgrn.shape[2:], dt)
    y_p, k_p, v_p, pool_p, conv_p, hgrn_p = _trunk(
        x_prompt, 0, zero_pool, zero_conv, zero_hgrn, None, None, None, w)
    past_len = page_table.shape[1] * cache_k.shape[2]
    to_pages = lambda c: jnp.transpose(c, (0, 1, 3, 4, 2)).reshape(c.shape[:2] + (GROUP_W, c.shape[2]))
    ck, cv = to_pages(cache_k), to_pages(cache_v)
    y_s, k_s, v_s, pool_s, conv_s, hgrn_s = _trunk(
        x_sample, past_len, state_pool, state_conv, state_hgrn, ck, cv, page_table, w)
    return (y_p, y_s, k_p, v_p, k_s, v_s, pool_p, pool_s, conv_p, conv_s, hgrn_p, hgrn_s)
```

```python
import functools

import jax
import jax.numpy as jnp
from jax import lax
from jax.experimental import pallas as pl
from jax.experimental.pallas import tpu as pltpu

F32 = jnp.float32
BF16 = jnp.bfloat16
HI = lax.Precision.HIGHEST

NORM_EPS = 1e-6
GROUP_W = 256
N_HEADS = 4
HEAD_DIM = GROUP_W // N_HEADS
N_IN_SPLITS = 10
POOL_WINDOWS = (2, 4, 8, 16)
POOL_PAD = 16
CONV_W = 31
CONV_PAD = 32
CONV_LEAD = CONV_PAD - (CONV_W - 1)
MOBA_BLOCK = 256
MOBA_TOPK = 3
ROPE_DIM = HEAD_DIM // 4
ROPE_THETA = 500000.0
ATTN_SCALE = HEAD_DIM ** -0.5
LOG2_E = 1.4426950408889634
NEG_BIG = -1e30
HGRN_CHUNK = 64
HGRN_SUB = 8
PAGES_PER_CHUNK = 64
MXU_TILE = 256
VMEM_LIMIT = 56 * 1024 * 1024


def _params(sem=None):
    return pltpu.CompilerParams(dimension_semantics=sem, vmem_limit_bytes=VMEM_LIMIT)


def _dot(a, b):
    return jnp.dot(a, b, preferred_element_type=F32)


def _dot_nt(a, b, precision=None):
    return lax.dot_general(a, b, (((1,), (1,)), ((), ())), precision=precision,
                           preferred_element_type=F32)


def _lane_head(shape):
    return lax.broadcasted_iota(jnp.int32, shape, len(shape) - 1) // HEAD_DIM


def _head_ones():
    r = lax.broadcasted_iota(jnp.int32, (GROUP_W, GROUP_W), 0) // HEAD_DIM
    c = lax.broadcasted_iota(jnp.int32, (GROUP_W, GROUP_W), 1) // HEAD_DIM
    return r == c


def _sigmoid(x):
    return 1.0 / (1.0 + jnp.exp(-x))


def _ffn_kernel(*refs, n_mix, ff_chunk):
    x_ref = refs[0]
    mix_refs = refs[1:1 + 2 * n_mix]
    g_ref, wg_ref, wu_ref, wd_ref, o_ref = refs[1 + 2 * n_mix:]
    x = x_ref[...]
    for idx in range(n_mix):
        x = x + _dot(mix_refs[idx][...].astype(BF16), mix_refs[n_mix + idx][...])
    ms = jnp.mean(x * x, axis=-1, keepdims=True)
    h = (x * lax.rsqrt(ms + NORM_EPS) * g_ref[...]).astype(BF16)
    acc = None
    for c0, c1 in zip((0,) + ff_chunk, ff_chunk + (wg_ref.shape[1],)):
        g = _dot(h, wg_ref[:, c0:c1])
        u = _dot(h, wu_ref[:, c0:c1])
        a = (g * _sigmoid(g) * u).astype(BF16)
        part = _dot(a, wd_ref[c0:c1, :])
        acc = part if acc is None else acc + part
    o_ref[...] = x + 0.5 * acc


def _resident(shape):
    return pl.BlockSpec(shape, lambda i: (0,) * len(shape), pipeline_mode=pl.Buffered(1))


def _ffn(x, gain, wg, wu, wd, mix=()):
    n, d = x.shape
    d_ff = wg.shape[1]
    tm = 512 if n % 512 == 0 else n
    ff_chunk = (-(-(d_ff // MXU_TILE) // 2) * MXU_TILE,) if d_ff % MXU_TILE == 0 and d_ff > MXU_TILE else ()
    tok = lambda i: (i, 0)
    in_specs = [pl.BlockSpec((tm, d), tok)]
    in_specs += [pl.BlockSpec((tm, y.shape[1]), tok) for y, _ in mix]
    in_specs += [_resident(w.shape) for _, w in mix]
    in_specs += [_resident((1, d)), _resident(wg.shape), _resident(wu.shape), _resident(wd.shape)]
    args = [x] + [y for y, _ in mix] + [w for _, w in mix] + [gain.reshape(1, d), wg, wu, wd]
    return pl.pallas_call(
        functools.partial(_ffn_kernel, n_mix=len(mix), ff_chunk=ff_chunk),
        grid=(n // tm,),
        in_specs=in_specs,
        out_specs=pl.BlockSpec((tm, d), tok),
        out_shape=jax.ShapeDtypeStruct((n, d), F32),
        compiler_params=_params(("parallel",)),
        name="ffn_merge" if mix else "ffn",
    )(*args)


def _spread_heads(x, fill):
    lane = lax.broadcasted_iota(jnp.int32, (x.shape[0], 128), 1)
    odd = pltpu.roll(x, GROUP_W - HEAD_DIM, axis=1)
    parts = (x[:, :128], odd[:, :128], x[:, 128:], odd[:, 128:])
    return jnp.concatenate([jnp.where(lane < HEAD_DIM, p, fill) for p in parts], axis=1)


def _inproj_kernel(*refs, prefill, tm, per_seq):
    x_ref, g_ref, w_ref, qg_ref, kg_ref, cos_ref, sin_ref = refs[:7]
    if prefill:
        up_ref, kt_ref, vt_ref, qp_ref, ka_ref, va_ref, cu_ref, hg_ref, km_ref = refs[7:]
    else:
        up_ref, q_ref, k_ref, v_ref, cu_ref, hg_ref = refs[7:]
    x = x_ref[...]
    ms = jnp.mean(x * x, axis=-1, keepdims=True)
    h = (x * lax.rsqrt(ms + NORM_EPS) * g_ref[...]).astype(BF16)

    def proj(c):
        return _dot(h, w_ref[:, c * GROUP_W:(c + 1) * GROUP_W])

    ones = _head_ones().astype(F32)
    cos = cos_ref[...]
    sin = sin_ref[...]
    d = lax.broadcasted_iota(jnp.int32, cos.shape, 1) % HEAD_DIM
    half = ROPE_DIM // 2

    def norm_rope(p, gain):
        ms_h = jnp.dot(p * p, ones, precision=HI, preferred_element_type=F32) * (1.0 / HEAD_DIM)
        y = p * lax.rsqrt(ms_h + NORM_EPS) * gain
        partner = jnp.where(d < half, pltpu.roll(y, GROUP_W - half, axis=1), pltpu.roll(y, half, axis=1))
        return y * cos + partner * sin

    up_ref[...] = proj(0)
    q = norm_rope(proj(1), qg_ref[...])
    k = norm_rope(proj(2), kg_ref[...])
    v = proj(3)
    cu_ref[...] = proj(4) * _sigmoid(proj(5))
    for c in range(4):
        hg_ref[:, c * GROUP_W:(c + 1) * GROUP_W] = proj(6 + c)
    if not prefill:
        q_ref[...] = q
        k_ref[...] = k
        v_ref[...] = v
        return
    kt_ref[0] = k.T
    vt_ref[0] = v.T
    lane = lax.broadcasted_iota(jnp.int32, (tm, 128), 1)
    row = lax.broadcasted_iota(jnp.int32, (tm, 128), 0)
    blk = ((pl.program_id(0) % per_seq) * tm + row) // MOBA_BLOCK
    zero = jnp.zeros((tm, 128), F32)
    kp = _spread_heads(k, zero)
    qp_ref[...] = _spread_heads(q, zero)
    ka_ref[...] = _spread_heads(k, jnp.where(lane - HEAD_DIM == blk, 1.0, 0.0)).astype(BF16)
    va_ref[...] = _spread_heads(v, jnp.where(lane == HEAD_DIM, 1.0, 0.0)).astype(BF16)
    for n in range(km_ref.shape[0]):
        km_ref[n] = jnp.mean(kp[n * MOBA_BLOCK:(n + 1) * MOBA_BLOCK, :], axis=0, keepdims=True)


def _rope_tables(t_len, q_pos0):
    half = ROPE_DIM // 2
    inv = ROPE_THETA ** (-jnp.arange(half, dtype=F32) * 2.0 / ROPE_DIM)
    pos = q_pos0 + jnp.arange(t_len, dtype=jnp.int32)
    ang = pos.astype(F32)[:, None] * inv
    c, s = jnp.cos(ang), jnp.sin(ang)
    rest = HEAD_DIM - ROPE_DIM
    cos_h = jnp.concatenate([c, c, jnp.ones((t_len, rest), F32)], axis=1)
    sin_h = jnp.concatenate([-s, s, jnp.zeros((t_len, rest), F32)], axis=1)
    return jnp.tile(cos_h, (1, N_HEADS)), jnp.tile(sin_h, (1, N_HEADS))


def _inproj(x, t_len, q_pos0, gain, w_in, q_gain, k_gain, prefill):
    n, d = x.shape
    if t_len % 512 == 0:
        tm = 512
    else:
        tm = t_len * max(r for r in range(1, n // t_len + 1) if (n // t_len) % r == 0 and t_len * r <= 512)
    assert not prefill or (tm % MOBA_BLOCK == 0 and t_len // MOBA_BLOCK <= 128 - HEAD_DIM)
    cos_t, sin_t = _rope_tables(t_len, q_pos0)
    if tm > t_len:
        cos_t, sin_t = jnp.tile(cos_t, (tm // t_len, 1)), jnp.tile(sin_t, (tm // t_len, 1))
    per_seq = max(t_len // tm, 1)
    tok = lambda i: (i, 0)
    tab = lambda i: (i % per_seq, 0)
    wide = 2 * GROUP_W
    grp = jax.ShapeDtypeStruct((n, GROUP_W), F32)
    gspec = pl.BlockSpec((tm, GROUP_W), tok)
    wspec = pl.BlockSpec((tm, wide), tok)
    x4 = jax.ShapeDtypeStruct((n, 4 * GROUP_W), F32)
    x4spec = pl.BlockSpec((tm, 4 * GROUP_W), tok)
    if prefill:
        kv_t = jax.ShapeDtypeStruct((n // t_len, GROUP_W, t_len), F32)
        kv_tspec = pl.BlockSpec((1, GROUP_W, tm), lambda i: (i // per_seq, 0, i % per_seq))
        out_shape = [grp, kv_t, kv_t, jax.ShapeDtypeStruct((n, wide), F32), jax.ShapeDtypeStruct((n, wide), BF16),
                     jax.ShapeDtypeStruct((n, wide), BF16), grp, x4,
                     jax.ShapeDtypeStruct((n // MOBA_BLOCK, 1, wide), F32)]
        out_specs = [gspec, kv_tspec, kv_tspec, wspec, wspec, wspec, gspec, x4spec,
                     pl.BlockSpec((tm // MOBA_BLOCK, 1, wide), lambda i: (i, 0, 0))]
    else:
        out_shape = [grp, grp, grp, grp, grp, x4]
        out_specs = [gspec] * 5 + [x4spec]
    return pl.pallas_call(
        functools.partial(_inproj_kernel, prefill=prefill, tm=tm, per_seq=per_seq),
        grid=(n // tm,),
        in_specs=[pl.BlockSpec((tm, d), tok), _resident((1, d)), _resident(w_in.shape),
                  _resident((1, GROUP_W)), _resident((1, GROUP_W)),
                  pl.BlockSpec((tm, GROUP_W), tab), pl.BlockSpec((tm, GROUP_W), tab)],
        out_specs=out_specs,
        out_shape=out_shape,
        compiler_params=_params(("parallel",)),
        name="inproj",
    )(x, gain.reshape(1, d), w_in, jnp.tile(q_gain, N_HEADS).reshape(1, GROUP_W),
      jnp.tile(k_gain, N_HEADS).reshape(1, GROUP_W), cos_t, sin_t)


def _poolconv_kernel(up_ref, cu_ref, pprev_ref, cprev_ref, pw_ref, ps_ref, cw_ref, cb_ref, lg_ref, lb_ref, cpw_ref,
                     ya_ref, yc_ref, pnew_ref, cnew_ref, pbuf, cbuf, zbuf, *, tt, q_pos0, rows):
    t = pl.program_id(1)

    @pl.when(t == 0)
    def _():
        pbuf[0:POOL_PAD, :] = pprev_ref[0]
        cbuf[0:CONV_PAD, :] = cprev_ref[0]
        cbuf[CONV_PAD + tt:CONV_PAD + tt + 8, :] = jnp.zeros((8, GROUP_W), F32)

    pbuf[POOL_PAD:POOL_PAD + tt, :] = up_ref[0]
    cbuf[CONV_PAD:CONV_PAD + tt, :] = cu_ref[0]

    for r0 in range(0, tt, rows):
        pos1 = (q_pos0 + t * tt + r0 + 1 + lax.broadcasted_iota(jnp.int32, (rows, 128), 0)).astype(F32)
        first = lax.broadcasted_iota(jnp.int32, (rows, 128), 1) < (GROUP_W // 4)
        pooled = []
        for half_i, (w_a, w_b) in enumerate(((POOL_WINDOWS[0], POOL_WINDOWS[1]), (POOL_WINDOWS[2], POOL_WINDOWS[3]))):
            lanes = slice(half_i * 128, (half_i + 1) * 128)
            u = pbuf[POOL_PAD + r0:POOL_PAD + r0 + rows, lanes]
            acc = u
            s_a = None
            for j in range(1, w_b):
                acc = acc + pbuf[POOL_PAD + r0 - j:POOL_PAD + r0 - j + rows, lanes]
                if j == w_a - 1:
                    s_a = acc
            avg = jnp.where(first, s_a / jnp.minimum(float(w_a), pos1), acc / jnp.minimum(float(w_b), pos1))
            pooled.append((avg - u).astype(BF16))
        pooled = jnp.concatenate(pooled, axis=1)
        ya_ref[0, r0:r0 + rows, :] = _dot(pooled, pw_ref[...]) * ps_ref[...]

        ext = rows + 8
        for s in range(8):
            z = None
            for a in range(-(-(CONV_W + CONV_LEAD) // 8)):
                j = 8 * a + s - CONV_LEAD
                if 0 <= j < CONV_W:
                    term = cbuf[r0 + 8 * a:r0 + 8 * a + ext, :] * cw_ref[j:j + 1, :]
                    z = term if z is None else z + term
            zbuf[s] = z
        acc = zbuf[0, 0:rows, :]
        for s in range(1, 8):
            acc = acc + zbuf[s, s:s + rows, :]
        y = acc + cb_ref[...]
        mu = jnp.mean(y, axis=-1, keepdims=True)
        var = jnp.mean(jnp.square(y - mu), axis=-1, keepdims=True)
        y = (y - mu) * lax.rsqrt(var + NORM_EPS) * lg_ref[...] + lb_ref[...]
        y = y * _sigmoid(y)
        yc_ref[0, r0:r0 + rows, :] = _dot(y.astype(BF16), cpw_ref[...])

    ptail = pbuf[tt:tt + POOL_PAD, :]
    ctail = cbuf[tt:tt + CONV_PAD, :]
    pnew_ref[0] = ptail
    cnew_ref[0] = ctail
    pbuf[0:POOL_PAD, :] = ptail
    cbuf[0:CONV_PAD, :] = ctail


def _poolconv(up, cu, pool_prev, conv_prev, q_pos0, pool_w, pool_scale, conv_w, conv_b, ln_g, ln_b, conv_pw):
    b, t_len, _ = up.shape
    tt = 512 if t_len % 512 == 0 else t_len
    rows = 64 if tt % 64 == 0 else tt
    pprev = jnp.pad(pool_prev, ((0, 0), (POOL_PAD - pool_prev.shape[1], 0), (0, 0)))
    cprev = jnp.pad(conv_prev, ((0, 0), (CONV_PAD - conv_prev.shape[1], 0), (0, 0)))
    pw_bd = jax.scipy.linalg.block_diag(*[pool_w[g] for g in range(pool_w.shape[0])]).astype(BF16)
    cw_pad = jnp.pad(conv_w, ((0, CONV_PAD - CONV_W), (0, 0)))
    seq = lambda i, j: (i, j, 0)
    per_b = lambda i, j: (i, 0, 0)
    const = lambda i, j: (0, 0)
    row = pl.BlockSpec((1, GROUP_W), const)
    ya, yc, pnew, cnew = pl.pallas_call(
        functools.partial(_poolconv_kernel, tt=tt, q_pos0=q_pos0, rows=rows),
        grid=(b, t_len // tt),
        in_specs=[pl.BlockSpec((1, tt, GROUP_W), seq), pl.BlockSpec((1, tt, GROUP_W), seq),
                  pl.BlockSpec((1, POOL_PAD, GROUP_W), per_b), pl.BlockSpec((1, CONV_PAD, GROUP_W), per_b),
                  pl.BlockSpec((GROUP_W, GROUP_W), const), row,
                  pl.BlockSpec((CONV_PAD, GROUP_W), const), row, row, row,
                  pl.BlockSpec((GROUP_W, GROUP_W), const)],
        out_specs=[pl.BlockSpec((1, tt, GROUP_W), seq), pl.BlockSpec((1, tt, GROUP_W), seq),
                   pl.BlockSpec((1, POOL_PAD, GROUP_W), per_b), pl.BlockSpec((1, CONV_PAD, GROUP_W), per_b)],
        out_shape=[jax.ShapeDtypeStruct((b, t_len, GROUP_W), F32), jax.ShapeDtypeStruct((b, t_len, GROUP_W), F32),
                   jax.ShapeDtypeStruct((b, POOL_PAD, GROUP_W), F32), jax.ShapeDtypeStruct((b, CONV_PAD, GROUP_W), F32)],
        scratch_shapes=[pltpu.VMEM((POOL_PAD + tt, GROUP_W), F32), pltpu.VMEM((CONV_PAD + tt + 8, GROUP_W), F32),
                        pltpu.VMEM((8, rows + 8, GROUP_W), F32)],
        compiler_params=_params(("parallel", "arbitrary")),
        name="poolconv",
    )(up, cu, pprev, cprev, pw_bd, pool_scale.reshape(1, GROUP_W), cw_pad, conv_b.reshape(1, GROUP_W),
      ln_g.reshape(1, GROUP_W), ln_b.reshape(1, GROUP_W), conv_pw.astype(BF16))
    return ya, yc, pnew[:, POOL_PAD - pool_prev.shape[1]:], cnew[:, CONV_PAD - conv_prev.shape[1]:]


def _stack_heads(q):
    head = _lane_head(q.shape)
    return jnp.concatenate([jnp.where(head == h, q, 0.0) for h in range(N_HEADS)], axis=0)


def _select_blocks(gate, n_past):
    lane = lax.broadcasted_iota(jnp.int32, gate.shape, 1)
    past = lane < n_past
    g = jnp.where(past, gate, NEG_BIG)
    sel = jnp.zeros(gate.shape, jnp.bool_)
    for _ in range(MOBA_TOPK):
        mx = jnp.max(g, axis=1, keepdims=True)
        idx = jnp.min(jnp.where(g == mx, lane, 1 << 20), axis=1, keepdims=True)
        pick = lane == idx
        sel = jnp.logical_or(sel, pick)
        g = jnp.where(pick, -jnp.inf, g)
    return jnp.logical_and(sel, past)


def _select_block_rows(gate_t, n_past, nblk):
    n = lax.broadcasted_iota(jnp.int32, gate_t.shape, 0)
    past = n < n_past
    g = jnp.where(past, gate_t, jnp.where(n < nblk, NEG_BIG, -jnp.inf))
    sel = jnp.zeros(gate_t.shape, jnp.bool_)
    for _ in range(MOBA_TOPK):
        mx = jnp.max(g, axis=0, keepdims=True)
        idx = jnp.min(jnp.where(g == mx, n, 1 << 20), axis=0, keepdims=True)
        pick = n == idx
        sel = jnp.logical_or(sel, pick)
        g = jnp.where(pick, -jnp.inf, g)
    return jnp.logical_and(sel, past)


def _moba_prefill_kernel(qp_ref, ka_ref, va_ref, km_ref, o_ref,
                         kmr_ref, qa_ref, s_ref, mx_ref, m_ref, acc_ref, *, nblk):
    i = pl.program_id(1)
    blk = MOBA_BLOCK
    kb = kmr_ref.shape[1]
    heads = [slice(h * 128, (h + 1) * 128) for h in range(N_HEADS)]

    @pl.when(i == 0)
    def _():
        kmr_ref[...] = jnp.zeros_like(kmr_ref)
        for h in range(N_HEADS):
            for n in range(nblk):
                kmr_ref[h, n:n + 1, :] = km_ref[0, n][:, heads[h]]

    n_row = lax.broadcasted_iota(jnp.int32, (kb, blk), 0)
    for h in range(N_HEADS):
        qh = qp_ref[0, :, heads[h]]
        gate_t = _dot_nt(kmr_ref[h], qh, precision=HI)
        sel = _select_block_rows(gate_t, i, nblk)
        bias_t = jnp.where(jnp.logical_or(sel, n_row == i), 0.0, NEG_BIG)
        bias_t = jnp.concatenate([jnp.zeros((HEAD_DIM, blk), F32), bias_t,
                                  jnp.zeros((128 - HEAD_DIM - kb, blk), F32)], axis=0)
        qa_ref[h] = (qh * (ATTN_SCALE * LOG2_E) + bias_t.T).astype(BF16)
        mx_ref[h] = jnp.full((blk, 128), NEG_BIG, F32)
        acc_ref[h] = jnp.zeros((blk, 128), F32)

    row = lax.broadcasted_iota(jnp.int32, (blk, blk), 0)
    col = lax.broadcasted_iota(jnp.int32, (blk, blk), 1)
    n_pairs = (i + 2) // 2

    def scores(jj, carry):
        for u in range(2):
            j = 2 * jj + u
            j0 = pl.multiple_of(j * blk, blk)
            allow = col + (j - i) * blk <= row
            for h in range(N_HEADS):
                s = jnp.where(allow, _dot_nt(qa_ref[h], ka_ref[0, pl.ds(j0, blk), heads[h]]), NEG_BIG)
                s_ref[h, j] = s
                mx_ref[h] = jnp.maximum(mx_ref[h], jnp.maximum(s[:, :128], s[:, 128:]))
        return carry

    lax.fori_loop(0, n_pairs, scores, 0)
    for h in range(N_HEADS):
        m_ref[h] = jnp.broadcast_to(jnp.max(mx_ref[h], axis=1, keepdims=True), (blk, 128))

    def weighted(jj, carry):
        for u in range(2):
            j = 2 * jj + u
            j0 = pl.multiple_of(j * blk, blk)
            for h in range(N_HEADS):
                s = s_ref[h, j]
                m = m_ref[h]
                p = jnp.concatenate([jnp.exp2(s[:, :128] - m), jnp.exp2(s[:, 128:] - m)], axis=1).astype(BF16)
                acc_ref[h] += _dot(p, va_ref[0, pl.ds(j0, blk), heads[h]])
        return carry

    lax.fori_loop(0, n_pairs, weighted, 0)
    lane = lax.broadcasted_iota(jnp.int32, (blk, 128), 1)
    for h in range(N_HEADS):
        acc = acc_ref[h]
        o_ref[0, :, heads[h]] = jnp.where(lane < HEAD_DIM, acc / acc[:, HEAD_DIM:HEAD_DIM + 1], 0.0)


def _moba_prefill(qp, ka, va, kmean):
    b, t_len, wide = qp.shape
    nblk = t_len // MOBA_BLOCK
    assert nblk % 2 == 0 and nblk <= 128 - HEAD_DIM
    kb = -(-nblk // 8) * 8
    blk = MOBA_BLOCK
    per_b = lambda i, j: (i, 0, 0)
    return pl.pallas_call(
        functools.partial(_moba_prefill_kernel, nblk=nblk),
        grid=(b, nblk),
        in_specs=[pl.BlockSpec((1, blk, wide), lambda i, j: (i, j, 0)),
                  pl.BlockSpec((1, t_len, wide), per_b, pipeline_mode=pl.Buffered(1)),
                  pl.BlockSpec((1, t_len, wide), per_b, pipeline_mode=pl.Buffered(1)),
                  pl.BlockSpec((1, nblk, 1, wide), lambda i, j: (i, 0, 0, 0))],
        out_specs=pl.BlockSpec((1, blk, wide), lambda i, j: (i, j, 0)),
        out_shape=jax.ShapeDtypeStruct((b, t_len, wide), F32),
        scratch_shapes=[pltpu.VMEM((N_HEADS, kb, 128), F32), pltpu.VMEM((N_HEADS, blk, 128), BF16),
                        pltpu.VMEM((N_HEADS, nblk, blk, blk), F32), pltpu.VMEM((N_HEADS, blk, 128), F32),
                        pltpu.VMEM((N_HEADS, blk, 128), F32), pltpu.VMEM((N_HEADS, blk, 128), F32)],
        compiler_params=_params(("parallel", "arbitrary")),
        name="moba_prefill",
    )(qp, ka, va, kmean)


def _moba_decode_kernel(pt_ref, q_ref, kn_ref, vn_ref, ck_ref, cv_ref, o_ref,
                        buf, sem, s_ref, p_ref, own_ref, *, layer, n_pages, tq):
    b = pl.program_id(0)
    ch = PAGES_PER_CHUNK
    n_chunks = n_pages // ch
    page = buf.shape[3]
    pages_per_blk = MOBA_BLOCK // page
    n_past = n_pages // pages_per_blk
    rows = N_HEADS * tq

    def page_copy(cache_ref, c, pg, slot, seq=b):
        return pltpu.make_async_copy(cache_ref.at[layer, pt_ref[seq, c * ch + pg]], buf.at[slot, pg], sem.at[slot])

    def start_chunk(cache_ref, c, slot, seq=b):
        for pg in range(ch):
            page_copy(cache_ref, c, pg, slot, seq).start()

    def wait_chunk(cache_ref, c, slot):
        for pg in range(ch):
            page_copy(cache_ref, c, pg, slot).wait()

    q4 = _stack_heads(q_ref[0]) * ATTN_SCALE
    q_hi = q4.astype(BF16)
    qq = jnp.concatenate([q_hi, (q4 - q_hi.astype(F32)).astype(BF16)], axis=0)

    def both(s2):
        return s2[:rows] + s2[rows:]

    @pl.when(b == 0)
    def _():
        start_chunk(ck_ref, 0, 0)

    def k_chunk(c, carry):
        slot = c % 2

        @pl.when(c + 1 < n_chunks)
        def _():
            start_chunk(ck_ref, c + 1, 1 - slot)

        @pl.when(c + 1 == n_chunks)
        def _():
            start_chunk(cv_ref, 0, 1 - slot)

        wait_chunk(ck_ref, c, slot)
        for pg in range(ch):
            s_ref[c * ch + pg] = both(_dot(qq, buf[slot, pg].astype(BF16)))
        return carry

    lax.fori_loop(0, n_chunks, k_chunk, 0)

    lane = lax.broadcasted_iota(jnp.int32, (rows, 128), 1)
    gate = jnp.zeros((rows, 128), F32)
    for n in range(n_past):
        tot = s_ref[n * pages_per_blk]
        for pg in range(1, pages_per_blk):
            tot = tot + s_ref[n * pages_per_blk + pg]
        gate = jnp.where(lane == n, jnp.sum(tot, axis=1, keepdims=True), gate)
    gate = gate * (1.0 / (MOBA_BLOCK * ATTN_SCALE))
    sel = _select_blocks(gate, n_past).astype(F32)

    own_ref[...] = jnp.zeros_like(own_ref)
    own_ref[0, 0:tq, :] = kn_ref[0]
    own_ref[1, 0:tq, :] = vn_ref[0]
    s_own = both(_dot_nt(qq, own_ref[0].astype(BF16)))
    s_own = jnp.where(lane <= lax.broadcasted_iota(jnp.int32, (rows, 128), 0) % tq, s_own, NEG_BIG)

    mx = s_own
    for n in range(n_past):
        ok = sel[:, n:n + 1] > 0.5
        for pg in range(n * pages_per_blk, (n + 1) * pages_per_blk):
            s_pg = jnp.where(ok, s_ref[pg], NEG_BIG)
            s_ref[pg] = s_pg
            mx = jnp.maximum(mx, s_pg)
    m = jnp.max(mx, axis=1, keepdims=True)
    p_own = jnp.exp(s_own - m)
    lsum = p_own
    for pg in range(n_pages):
        p = jnp.exp(s_ref[pg] - m)
        lsum = lsum + p
        p_ref[pg] = p.astype(BF16)
    l = jnp.sum(lsum, axis=1, keepdims=True)

    def v_chunk(c, acc):
        slot = (n_chunks + c) % 2

        @pl.when(c + 1 < n_chunks)
        def _():
            start_chunk(cv_ref, c + 1, 1 - slot)

        @pl.when(jnp.logical_and(c + 1 == n_chunks, b + 1 < pl.num_programs(0)))
        def _():
            start_chunk(ck_ref, 0, 1 - slot, seq=b + 1)

        wait_chunk(cv_ref, c, slot)
        for pg in range(ch):
            acc = acc + _dot_nt(p_ref[c * ch + pg], buf[slot, pg].astype(BF16))
        return acc

    acc = lax.fori_loop(0, n_chunks, v_chunk, _dot(p_own.astype(BF16), own_ref[1].astype(BF16)))
    acc = acc / l
    head = _lane_head((tq, GROUP_W))
    out = jnp.zeros((tq, GROUP_W), F32)
    for h in range(N_HEADS):
        out = out + jnp.where(head == h, acc[h * tq:(h + 1) * tq, :], 0.0)
    o_ref[0] = out


def _moba_decode(q, k_new, v_new, cache_k, cache_v, page_table, layer):
    b, tq, _ = q.shape
    n_pages = page_table.shape[1]
    page = cache_k.shape[3]
    assert page == 128 and MOBA_BLOCK % page == 0 and n_pages % (MOBA_BLOCK // page) == 0
    assert n_pages % PAGES_PER_CHUNK == 0 and tq <= 128 and n_pages * page // MOBA_BLOCK <= 128
    rows = N_HEADS * tq
    per_b = lambda i, pt: (i, 0, 0)
    grid_spec = pltpu.PrefetchScalarGridSpec(
        num_scalar_prefetch=1,
        grid=(b,),
        in_specs=[pl.BlockSpec((1, tq, GROUP_W), per_b), pl.BlockSpec((1, tq, GROUP_W), per_b),
                  pl.BlockSpec((1, tq, GROUP_W), per_b),
                  pl.BlockSpec(memory_space=pl.ANY), pl.BlockSpec(memory_space=pl.ANY)],
        out_specs=pl.BlockSpec((1, tq, GROUP_W), per_b),
        scratch_shapes=[pltpu.VMEM((2, PAGES_PER_CHUNK, GROUP_W, page), F32),
                        pltpu.SemaphoreType.DMA((2,)),
                        pltpu.VMEM((n_pages, rows, page), F32), pltpu.VMEM((n_pages, rows, page), BF16),
                        pltpu.VMEM((2, 128, GROUP_W), F32)],
    )
    return pl.pallas_call(
        functools.partial(_moba_decode_kernel, layer=layer, n_pages=n_pages, tq=tq),
        grid_spec=grid_spec,
        out_shape=jax.ShapeDtypeStruct((b, tq, GROUP_W), F32),
        compiler_params=_params(("arbitrary",)),
        name="moba_decode",
    )(page_table, q, k_new, v_new, cache_k, cache_v)


def _hgrn_kernel(x_ref, st0_ref, lb_ref, ng_ref, y_ref, stout_ref, st_ref, qst_ref, upd_ref, dec_ref,
                 *, tt, chunk, sub):
    t = pl.program_id(1)
    n_sub = chunk // sub

    @pl.when(t == 0)
    def _():
        st_ref[...] = st0_ref[0]

    lb = lb_ref[...]
    head = _lane_head((1, GROUP_W))
    hmask = [head == h for h in range(N_HEADS)]
    bd = _head_ones()
    ones_b = bd.astype(BF16)
    ones_f = bd.astype(F32)
    tri = (lax.broadcasted_iota(jnp.int32, (chunk, chunk), 1)
           <= lax.broadcasted_iota(jnp.int32, (chunk, chunk), 0)).astype(F32)
    t_ge = [lax.broadcasted_iota(jnp.int32, (sub, GROUP_W), 0) >= s for s in range(sub)]

    n_chunks = tt // chunk
    if n_sub > 1:
        n_used = sub * (n_sub * (n_sub - 1) // 2)
        n_col = -(-n_used // 128) * 128
        col = lax.broadcasted_iota(jnp.int32, (N_HEADS * chunk, n_col), 1)
        row_sub = (lax.broadcasted_iota(jnp.int32, (N_HEADS * chunk, n_col), 0) % chunk) // sub
        keep = jnp.zeros((N_HEADS * chunk, n_col), jnp.bool_)
        c0 = 0
        for i in range(1, n_sub):
            keep = jnp.logical_or(keep, jnp.logical_and(row_sub == i, jnp.logical_and(col >= c0, col < c0 + i * sub)))
            c0 += i * sub

    def local_part(ci):
        r0 = ci * chunk if isinstance(ci, int) else pl.multiple_of(ci * chunk, chunk)
        hq = x_ref[0, pl.ds(r0, chunk), 0 * GROUP_W:1 * GROUP_W]
        hf = x_ref[0, pl.ds(r0, chunk), 1 * GROUP_W:2 * GROUP_W]
        hi = x_ref[0, pl.ds(r0, chunk), 2 * GROUP_W:3 * GROUP_W]
        f = lb + (1.0 - lb) * _sigmoid(hf)
        kin = 1.0 - f
        bc = jnp.dot(tri, jnp.log2(f), precision=HI, preferred_element_type=F32)
        b_last = bc[chunk - 1:chunk, :]
        qst_ref[pl.ds(r0, chunk), :] = hq * jnp.exp2(bc)
        k_state = (kin * jnp.exp2(b_last - bc)).astype(BF16)
        upd_ref[ci] = jnp.where(bd, _dot(hi.T.astype(BF16), k_state), 0.0)
        dec_ref[ci] = jnp.exp2(b_last)

        o = None
        if n_sub > 1:
            bq = jnp.concatenate(
                [jnp.zeros((sub, GROUP_W), F32)]
                + [jnp.broadcast_to(bc[i * sub - 1:i * sub, :], (sub, GROUP_W)) for i in range(1, n_sub)], axis=0)
            q_off = hq * jnp.exp2(bc - bq)
            q4 = jnp.concatenate([jnp.where(hmask[h], q_off, 0.0) for h in range(N_HEADS)], axis=0).astype(BF16)
            pad = [jnp.zeros((n_col - n_used, GROUP_W), F32)] if n_col > n_used else []
            k_st = jnp.concatenate(
                [kin[0:i * sub, :] * jnp.exp2(bc[i * sub - 1:i * sub, :] - bc[0:i * sub, :])
                 for i in range(1, n_sub)] + pad, axis=0).astype(BF16)
            v_st = jnp.concatenate([hi[0:i * sub, :] for i in range(1, n_sub)] + pad, axis=0).astype(BF16)
            att = jnp.where(keep, _dot_nt(q4, k_st), 0.0).astype(BF16)
            res = _dot(att, v_st)
            o = jnp.where(hmask[0], res[0:chunk], 0.0)
            for h in range(1, N_HEADS):
                o = o + jnp.where(hmask[h], res[h * chunk:(h + 1) * chunk], 0.0)

        d_rows = []
        for i in range(n_sub):
            rs = slice(i * sub, (i + 1) * sub)
            q_i, bc_i, k_i = hq[rs], bc[rs], kin[rs]
            for s in range(sub):
                e = jnp.exp2(jnp.where(t_ge[s], bc_i - bc_i[s:s + 1, :], NEG_BIG))
                d_rows.append(q_i * e * k_i[s:s + 1, :])
        a = _dot(jnp.concatenate(d_rows, axis=0).astype(BF16), ones_b)
        o_diag = []
        for i in range(n_sub):
            v_i = hi[i * sub:(i + 1) * sub]
            acc = a[i * sub * sub:i * sub * sub + sub, :] * v_i[0:1, :]
            for s in range(1, sub):
                base = (i * sub + s) * sub
                acc = acc + a[base:base + sub, :] * v_i[s:s + 1, :]
            o_diag.append(acc)
        o_diag = jnp.concatenate(o_diag, axis=0) if n_sub > 1 else o_diag[0]
        y_ref[0, pl.ds(r0, chunk), :] = o_diag if o is None else o + o_diag

    def carried_part(ci, carry):
        r0 = pl.multiple_of(ci * chunk, chunk)
        st = st_ref[...]
        y_ref[0, pl.ds(r0, chunk), :] += _dot_nt(qst_ref[pl.ds(r0, chunk), :].astype(BF16), st.astype(BF16))
        st_ref[...] = st * dec_ref[ci] + upd_ref[ci]
        return carry

    if n_chunks % 2 == 0:
        def pair(cp, carry):
            local_part(2 * cp)
            local_part(2 * cp + 1)
            return carry
        lax.fori_loop(0, n_chunks // 2, pair, 0)
    else:
        for ci in range(n_chunks):
            local_part(ci)
    lax.fori_loop(0, n_chunks, carried_part, 0)

    o = y_ref[0]
    hg = x_ref[0, :, 3 * GROUP_W:4 * GROUP_W]
    ms = jnp.dot(o * o, ones_f, precision=HI, preferred_element_type=F32) * (1.0 / HEAD_DIM)
    y_ref[0] = o * lax.rsqrt(ms + NORM_EPS) * ng_ref[...] * (hg * _sigmoid(hg))

    @pl.when(t == pl.num_programs(1) - 1)
    def _():
        stout_ref[0] = st_ref[...]


def _hgrn(x4, state, lb, norm_g):
    b, t_len, _ = x4.shape
    chunk = HGRN_CHUNK if t_len % HGRN_CHUNK == 0 else t_len
    sub = HGRN_SUB if chunk % HGRN_SUB == 0 else chunk
    tt = 512 if t_len % 512 == 0 else (chunk if t_len == chunk else t_len)
    eye = jnp.eye(N_HEADS, dtype=state.dtype)
    st0 = jnp.einsum('bhkv,hg->bhvgk', state, eye).reshape(b, GROUP_W, GROUP_W)
    y, st = pl.pallas_call(
        functools.partial(_hgrn_kernel, tt=tt, chunk=chunk, sub=sub),
        grid=(b, t_len // tt),
        in_specs=[pl.BlockSpec((1, tt, 4 * GROUP_W), lambda i, j: (i, j, 0)),
                  pl.BlockSpec((1, GROUP_W, GROUP_W), lambda i, j: (i, 0, 0)),
                  pl.BlockSpec((1, GROUP_W), lambda i, j: (0, 0)),
                  pl.BlockSpec((1, GROUP_W), lambda i, j: (0, 0))],
        out_specs=[pl.BlockSpec((1, tt, GROUP_W), lambda i, j: (i, j, 0)),
                   pl.BlockSpec((1, GROUP_W, GROUP_W), lambda i, j: (i, 0, 0))],
        out_shape=[jax.ShapeDtypeStruct((b, t_len, GROUP_W), F32),
                   jax.ShapeDtypeStruct((b, GROUP_W, GROUP_W), F32)],
        scratch_shapes=[pltpu.VMEM((GROUP_W, GROUP_W), F32), pltpu.VMEM((tt, GROUP_W), F32),
                        pltpu.VMEM((tt // chunk, GROUP_W, GROUP_W), F32), pltpu.VMEM((tt // chunk, 1, GROUP_W), F32)],
        compiler_params=_params(("parallel", "arbitrary")),
        name="hgrn",
    )(x4, st0, lb.reshape(1, GROUP_W), jnp.tile(norm_g, N_HEADS).reshape(1, GROUP_W))
    st5 = st.reshape(b, N_HEADS, HEAD_DIM, N_HEADS, HEAD_DIM)
    new_state = jnp.stack([st5[:, h, :, h, :] for h in range(N_HEADS)], axis=1)
    return y, jnp.swapaxes(new_state, 2, 3)


def _trunk(x, q_pos0, pool_st, conv_st, hgrn_st, cache_k, cache_v, page_table, w):
    b, t_len, d = x.shape
    n = b * t_len
    depth = w["ln_ffn1"].shape[0]
    lbs = jax.nn.softmax(w["hgrn_lower_bounds"].astype(F32), axis=0)
    lbs = jnp.cumsum(lbs, axis=0) - lbs[0]
    prefill = cache_k is None
    xf = x.reshape(n, d)
    ks, vs, pools, convs, hgrns = [], [], [], [], []
    for l in range(depth):
        xf = _ffn(xf, w["ln_ffn1"][l], w["ffn1_w_gate"][l], w["ffn1_w_up"][l], w["ffn1_w_down"][l])
        outs = _inproj(xf, t_len, q_pos0, w["ln_mix"][l], w["w_in"][l], w["q_norm"][l], w["k_norm"][l], prefill)
        seq = lambda a: a.reshape(b, t_len, a.shape[-1])
        wo = w["w_out"][l]
        wo_parts = [wo[g * GROUP_W:(g + 1) * GROUP_W] for g in range(4)]
        if prefill:
            up, kt, vt, qp, ka, va, cu, x4, kmean = outs
            k, v = (jnp.transpose(a.reshape(b, N_HEADS, HEAD_DIM, t_len), (0, 3, 1, 2)) for a in (kt, vt))
            yb = _moba_prefill(seq(qp), seq(ka), seq(va), kmean.reshape(b, t_len // MOBA_BLOCK, 1, 2 * GROUP_W))
            wo_parts[1] = jnp.pad(wo_parts[1].reshape(N_HEADS, HEAD_DIM, d),
                                  ((0, 0), (0, 128 - HEAD_DIM), (0, 0))).reshape(N_HEADS * 128, d)
        else:
            up, q, k, v, cu, x4 = outs
            yb = _moba_decode(seq(q), seq(k), seq(v), cache_k, cache_v, page_table, l)
            k, v = (a.reshape(b, t_len, N_HEADS, HEAD_DIM) for a in (k, v))
        ya, yc, pool_new, conv_new = _poolconv(
            seq(up), seq(cu), pool_st[l], conv_st[l], q_pos0, w["pool_w"][l], w["pool_scale"][l],
            w["conv_w"][l], w["conv_b"][l], w["conv_ln_g"][l], w["conv_ln_b"][l], w["conv_pw"][l])
        yd, s_new = _hgrn(seq(x4), hgrn_st[l], lbs[l], w["hgrn_norm"][l])
        flat = lambda a: a.reshape(n, a.shape[-1])
        xf = _ffn(xf, w["ln_ffn2"][l], w["ffn2_w_gate"][l], w["ffn2_w_up"][l], w["ffn2_w_down"][l],
                  mix=tuple(zip((flat(ya), flat(yb), flat(yc), flat(yd)), wo_parts)))
        ks.append(k)
        vs.append(v)
        pools.append(pool_new)
        convs.append(conv_new)
        hgrns.append(s_new)
    return (xf.reshape(b, t_len, d), jnp.stack(ks), jnp.stack(vs), jnp.stack(pools), jnp.stack(convs),
            jnp.stack(hgrns))


def kernel(x_prompt, x_sample, cache_k, cache_v, page_table, state_pool, state_conv, state_hgrn, ln_ffn1, ffn1_w_gate, ffn1_w_up, ffn1_w_down, ln_mix, w_in, w_out, pool_w, pool_scale, q_norm, k_norm, conv_w, conv_b, conv_ln_g, conv_ln_b, conv_pw, hgrn_lower_bounds, hgrn_norm, ln_ffn2, ffn2_w_gate, ffn2_w_up, ffn2_w_down):
    w = dict(ln_ffn1=ln_ffn1, ln_mix=ln_mix, ln_ffn2=ln_ffn2, pool_w=pool_w, pool_scale=pool_scale,
             q_norm=q_norm, k_norm=k_norm, conv_w=conv_w, conv_b=conv_b, conv_ln_g=conv_ln_g,
             conv_ln_b=conv_ln_b, conv_pw=conv_pw, hgrn_lower_bounds=hgrn_lower_bounds, hgrn_norm=hgrn_norm)
    for name, arr in (("ffn1_w_gate", ffn1_w_gate), ("ffn1_w_up", ffn1_w_up), ("ffn1_w_down", ffn1_w_down),
                      ("w_in", w_in), ("w_out", w_out), ("ffn2_w_gate", ffn2_w_gate), ("ffn2_w_up", ffn2_w_up),
                      ("ffn2_w_down", ffn2_w_down)):
        w[name] = arr.astype(BF16)
    depth = ln_ffn1.shape[0]
    bp = x_prompt.shape[0]
    dt = x_prompt.dtype
    zero_pool = jnp.zeros((depth, bp) + state_pool.shape[2:], dt)
    zero_conv = jnp.zeros((depth, bp) + state_conv.shape[2:], dt)
    zero_hgrn = jnp.zeros((depth, bp) + state_hgrn.shape[2:], dt)
    y_p, k_p, v_p, pool_p, conv_p, hgrn_p = _trunk(
        x_prompt, 0, zero_pool, zero_conv, zero_hgrn, None, None, None, w)
    past_len = page_table.shape[1] * cache_k.shape[2]
    to_pages = lambda c: jnp.transpose(c, (0, 1, 3, 4, 2)).reshape(c.shape[:2] + (GROUP_W, c.shape[2]))
    ck, cv = to_pages(cache_k), to_pages(cache_v)
    y_s, k_s, v_s, pool_s, conv_s, hgrn_s = _trunk(
        x_sample, past_len, state_pool, state_conv, state_hgrn, ck, cv, page_table, w)
    return (y_p, y_s, k_p, v_p, k_s, v_s, pool_p, pool_s, conv_p, conv_s, hgrn_p, hgrn_s)
```

```python
import functools

import jax
import jax.numpy as jnp
from jax import lax
from jax.experimental import pallas as pl
from jax.experimental.pallas import tpu as pltpu

F32 = jnp.float32
BF16 = jnp.bfloat16
HI = lax.Precision.HIGHEST

NORM_EPS = 1e-6
GROUP_W = 256
N_HEADS = 4
HEAD_DIM = GROUP_W // N_HEADS
N_IN_SPLITS = 10
POOL_WINDOWS = (2, 4, 8, 16)
POOL_PAD = 16
CONV_W = 31
CONV_PAD = 32
CONV_LEAD = CONV_PAD - (CONV_W - 1)
MOBA_BLOCK = 256
MOBA_TOPK = 3
ROPE_DIM = HEAD_DIM // 4
ROPE_THETA = 500000.0
ATTN_SCALE = HEAD_DIM ** -0.5
LOG2_E = 1.4426950408889634
NEG_BIG = -1e30
HGRN_CHUNK = 64
HGRN_SUB = 8
PAGES_PER_CHUNK = 64
MXU_TILE = 256
VMEM_LIMIT = 56 * 1024 * 1024


def _params(sem=None):
    return pltpu.CompilerParams(dimension_semantics=sem, vmem_limit_bytes=VMEM_LIMIT)


def _dot(a, b):
    return jnp.dot(a, b, preferred_element_type=F32)


def _dot_nt(a, b, precision=None):
    return lax.dot_general(a, b, (((1,), (1,)), ((), ())), precision=precision,
                           preferred_element_type=F32)


def _lane_head(shape):
    return lax.broadcasted_iota(jnp.int32, shape, len(shape) - 1) // HEAD_DIM


def _head_ones():
    r = lax.broadcasted_iota(jnp.int32, (GROUP_W, GROUP_W), 0) // HEAD_DIM
    c = lax.broadcasted_iota(jnp.int32, (GROUP_W, GROUP_W), 1) // HEAD_DIM
    return r == c


def _sigmoid(x):
    return 1.0 / (1.0 + jnp.exp(-x))


def _ffn_kernel(*refs, n_mix, ff_chunk):
    x_ref = refs[0]
    mix_refs = refs[1:1 + 2 * n_mix]
    g_ref, wg_ref, wu_ref, wd_ref, o_ref = refs[1 + 2 * n_mix:]
    x = x_ref[...]
    for idx in range(n_mix):
        x = x + _dot(mix_refs[idx][...].astype(BF16), mix_refs[n_mix + idx][...])
    ms = jnp.mean(x * x, axis=-1, keepdims=True)
    h = (x * lax.rsqrt(ms + NORM_EPS) * g_ref[...]).astype(BF16)
    acc = None
    for c0, c1 in zip((0,) + ff_chunk, ff_chunk + (wg_ref.shape[1],)):
        g = _dot(h, wg_ref[:, c0:c1])
        u = _dot(h, wu_ref[:, c0:c1])
        a = (g * _sigmoid(g) * u).astype(BF16)
        part = _dot(a, wd_ref[c0:c1, :])
        acc = part if acc is None else acc + part
    o_ref[...] = x + 0.5 * acc


def _resident(shape):
    return pl.BlockSpec(shape, lambda i: (0,) * len(shape), pipeline_mode=pl.Buffered(1))


def _ffn(x, gain, wg, wu, wd, mix=()):
    n, d = x.shape
    d_ff = wg.shape[1]
    tm = 512 if n % 512 == 0 else n
    ff_chunk = (-(-(d_ff // MXU_TILE) // 2) * MXU_TILE,) if d_ff % MXU_TILE == 0 and d_ff > MXU_TILE else ()
    tok = lambda i: (i, 0)
    in_specs = [pl.BlockSpec((tm, d), tok)]
    in_specs += [pl.BlockSpec((tm, y.shape[1]), tok) for y, _ in mix]
    in_specs += [_resident(w.shape) for _, w in mix]
    in_specs += [_resident((1, d)), _resident(wg.shape), _resident(wu.shape), _resident(wd.shape)]
    args = [x] + [y for y, _ in mix] + [w for _, w in mix] + [gain.reshape(1, d), wg, wu, wd]
    return pl.pallas_call(
        functools.partial(_ffn_kernel, n_mix=len(mix), ff_chunk=ff_chunk),
        grid=(n // tm,),
        in_specs=in_specs,
        out_specs=pl.BlockSpec((tm, d), tok),
        out_shape=jax.ShapeDtypeStruct((n, d), F32),
        compiler_params=_params(("parallel",)),
        name="ffn_merge" if mix else "ffn",
    )(*args)


def _spread_heads(x, fill):
    lane = lax.broadcasted_iota(jnp.int32, (x.shape[0], 128), 1)
    odd = pltpu.roll(x, GROUP_W - HEAD_DIM, axis=1)
    parts = (x[:, :128], odd[:, :128], x[:, 128:], odd[:, 128:])
    return jnp.concatenate([jnp.where(lane < HEAD_DIM, p, fill) for p in parts], axis=1)


def _inproj_kernel(*refs, prefill, tm, per_seq, n_prev):
    x_ref, g_ref, w_ref, qg_ref, kg_ref, cos_ref, sin_ref = refs[:7]
    if prefill:
        ktp_ref, vtp_ref = refs[7:9] if n_prev else (None, None)
        up_ref, kt_ref, vt_ref, qp_ref, ka_ref, va_ref, cu_ref, hg_ref, km_ref = refs[7 + (2 if n_prev else 0):]
    else:
        up_ref, q_ref, k_ref, v_ref, cu_ref, hg_ref = refs[7:]
    x = x_ref[...]
    ms = jnp.mean(x * x, axis=-1, keepdims=True)
    h = (x * lax.rsqrt(ms + NORM_EPS) * g_ref[...]).astype(BF16)

    def proj(c):
        return _dot(h, w_ref[:, c * GROUP_W:(c + 1) * GROUP_W])

    ones = _head_ones().astype(F32)
    cos = cos_ref[...]
    sin = sin_ref[...]
    d = lax.broadcasted_iota(jnp.int32, cos.shape, 1) % HEAD_DIM
    half = ROPE_DIM // 2

    def norm_rope(p, gain):
        ms_h = jnp.dot(p * p, ones, precision=HI, preferred_element_type=F32) * (1.0 / HEAD_DIM)
        y = p * lax.rsqrt(ms_h + NORM_EPS) * gain
        partner = jnp.where(d < half, pltpu.roll(y, GROUP_W - half, axis=1), pltpu.roll(y, half, axis=1))
        return y * cos + partner * sin

    up_ref[...] = proj(0)
    q = norm_rope(proj(1), qg_ref[...])
    k = norm_rope(proj(2), kg_ref[...])
    v = proj(3)
    cu_ref[...] = proj(4) * _sigmoid(proj(5))
    for c in range(4):
        hg_ref[:, c * GROUP_W:(c + 1) * GROUP_W] = proj(6 + c)
    if not prefill:
        q_ref[...] = q
        k_ref[...] = k
        v_ref[...] = v
        return
    if n_prev:
        kt_ref[0:n_prev] = ktp_ref[...]
        vt_ref[0:n_prev] = vtp_ref[...]
    kt_ref[n_prev, 0] = k.T
    vt_ref[n_prev, 0] = v.T
    lane = lax.broadcasted_iota(jnp.int32, (tm, 128), 1)
    row = lax.broadcasted_iota(jnp.int32, (tm, 128), 0)
    blk = ((pl.program_id(0) % per_seq) * tm + row) // MOBA_BLOCK
    zero = jnp.zeros((tm, 128), F32)
    kp = _spread_heads(k, zero)
    qp_ref[...] = _spread_heads(q, zero)
    ka_ref[...] = _spread_heads(k, jnp.where(lane - HEAD_DIM == blk, 1.0, 0.0)).astype(BF16)
    va_ref[...] = _spread_heads(v, jnp.where(lane == HEAD_DIM, 1.0, 0.0)).astype(BF16)
    for n in range(km_ref.shape[0]):
        km_ref[n] = jnp.mean(kp[n * MOBA_BLOCK:(n + 1) * MOBA_BLOCK, :], axis=0, keepdims=True)


def _rope_tables(t_len, q_pos0):
    half = ROPE_DIM // 2
    inv = ROPE_THETA ** (-jnp.arange(half, dtype=F32) * 2.0 / ROPE_DIM)
    pos = q_pos0 + jnp.arange(t_len, dtype=jnp.int32)
    ang = pos.astype(F32)[:, None] * inv
    c, s = jnp.cos(ang), jnp.sin(ang)
    rest = HEAD_DIM - ROPE_DIM
    cos_h = jnp.concatenate([c, c, jnp.ones((t_len, rest), F32)], axis=1)
    sin_h = jnp.concatenate([-s, s, jnp.zeros((t_len, rest), F32)], axis=1)
    return jnp.tile(cos_h, (1, N_HEADS)), jnp.tile(sin_h, (1, N_HEADS))


def _inproj(x, t_len, q_pos0, gain, w_in, q_gain, k_gain, prefill, kv_prev=None):
    n, d = x.shape
    if t_len % 512 == 0:
        tm = 512
    else:
        tm = t_len * max(r for r in range(1, n // t_len + 1) if (n // t_len) % r == 0 and t_len * r <= 512)
    assert not prefill or (tm % MOBA_BLOCK == 0 and t_len // MOBA_BLOCK <= 128 - HEAD_DIM)
    cos_t, sin_t = _rope_tables(t_len, q_pos0)
    if tm > t_len:
        cos_t, sin_t = jnp.tile(cos_t, (tm // t_len, 1)), jnp.tile(sin_t, (tm // t_len, 1))
    per_seq = max(t_len // tm, 1)
    tok = lambda i: (i, 0)
    tab = lambda i: (i % per_seq, 0)
    wide = 2 * GROUP_W
    grp = jax.ShapeDtypeStruct((n, GROUP_W), F32)
    gspec = pl.BlockSpec((tm, GROUP_W), tok)
    wspec = pl.BlockSpec((tm, wide), tok)
    x4 = jax.ShapeDtypeStruct((n, 4 * GROUP_W), F32)
    x4spec = pl.BlockSpec((tm, 4 * GROUP_W), tok)
    n_prev = kv_prev[0].shape[0] if kv_prev is not None else 0
    prev_args, prev_specs = [], []
    if prefill:
        kv_map = lambda i: (0, i // per_seq, 0, i % per_seq)
        kv_t = jax.ShapeDtypeStruct((n_prev + 1, n // t_len, GROUP_W, t_len), F32)
        kv_tspec = pl.BlockSpec((n_prev + 1, 1, GROUP_W, tm), kv_map)
        if n_prev:
            prev_args = list(kv_prev)
            prev_specs = [pl.BlockSpec((n_prev, 1, GROUP_W, tm), kv_map)] * 2
        out_shape =[grp, kv_t, kv_t, jax.ShapeDtypeStruct((n, wide), F32), jax.ShapeDtypeStruct((n, wide), BF16),
                     jax.ShapeDtypeStruct((n, wide), BF16), grp, x4,
                     jax.ShapeDtypeStruct((n // MOBA_BLOCK, 1, wide), F32)]
        out_specs = [gspec, kv_tspec, kv_tspec, wspec, wspec, wspec, gspec, x4spec,
                     pl.BlockSpec((tm // MOBA_BLOCK, 1, wide), lambda i: (i, 0, 0))]
    else:
        out_shape = [grp, grp, grp, grp, grp, x4]
        out_specs = [gspec] * 5 + [x4spec]
    return pl.pallas_call(
        functools.partial(_inproj_kernel, prefill=prefill, tm=tm, per_seq=per_seq, n_prev=n_prev),
        grid=(n // tm,),
        in_specs=[pl.BlockSpec((tm, d), tok), _resident((1, d)), _resident(w_in.shape),
                  _resident((1, GROUP_W)), _resident((1, GROUP_W)),
                  pl.BlockSpec((tm, GROUP_W), tab), pl.BlockSpec((tm, GROUP_W), tab)] + prev_specs,
        out_specs=out_specs,
        out_shape=out_shape,
        compiler_params=_params(("parallel",)),
        name="inproj",
    )(x, gain.reshape(1, d), w_in, jnp.tile(q_gain, N_HEADS).reshape(1, GROUP_W),
      jnp.tile(k_gain, N_HEADS).reshape(1, GROUP_W), cos_t, sin_t, *prev_args)


def _poolconv_kernel(up_ref, cu_ref, pprev_ref, cprev_ref, pw_ref, ps_ref, cw_ref, cb_ref, lg_ref, lb_ref, cpw_ref,
                     ya_ref, yc_ref, pnew_ref, cnew_ref, pbuf, cbuf, zbuf, *, tt, q_pos0, rows):
    t = pl.program_id(1)

    @pl.when(t == 0)
    def _():
        pbuf[0:POOL_PAD, :] = pprev_ref[0]
        cbuf[0:CONV_PAD, :] = cprev_ref[0]
        cbuf[CONV_PAD + tt:CONV_PAD + tt + 8, :] = jnp.zeros((8, GROUP_W), F32)

    pbuf[POOL_PAD:POOL_PAD + tt, :] = up_ref[0]
    cbuf[CONV_PAD:CONV_PAD + tt, :] = cu_ref[0]

    for r0 in range(0, tt, rows):
        pos1 = (q_pos0 + t * tt + r0 + 1 + lax.broadcasted_iota(jnp.int32, (rows, 128), 0)).astype(F32)
        first = lax.broadcasted_iota(jnp.int32, (rows, 128), 1) < (GROUP_W // 4)
        pooled = []
        for half_i, (w_a, w_b) in enumerate(((POOL_WINDOWS[0], POOL_WINDOWS[1]), (POOL_WINDOWS[2], POOL_WINDOWS[3]))):
            lanes = slice(half_i * 128, (half_i + 1) * 128)
            u = pbuf[POOL_PAD + r0:POOL_PAD + r0 + rows, lanes]
            acc = u
            s_a = None
            for j in range(1, w_b):
                acc = acc + pbuf[POOL_PAD + r0 - j:POOL_PAD + r0 - j + rows, lanes]
                if j == w_a - 1:
                    s_a = acc
            avg = jnp.where(first, s_a / jnp.minimum(float(w_a), pos1), acc / jnp.minimum(float(w_b), pos1))
            pooled.append((avg - u).astype(BF16))
        pooled = jnp.concatenate(pooled, axis=1)
        ya_ref[0, r0:r0 + rows, :] = _dot(pooled, pw_ref[...]) * ps_ref[...]

        ext = rows + 8
        for s in range(8):
            z = None
            for a in range(-(-(CONV_W + CONV_LEAD) // 8)):
                j = 8 * a + s - CONV_LEAD
                if 0 <= j < CONV_W:
                    term = cbuf[r0 + 8 * a:r0 + 8 * a + ext, :] * cw_ref[j:j + 1, :]
                    z = term if z is None else z + term
            zbuf[s] = z
        acc = zbuf[0, 0:rows, :]
        for s in range(1, 8):
            acc = acc + zbuf[s, s:s + rows, :]
        y = acc + cb_ref[...]
        mu = jnp.mean(y, axis=-1, keepdims=True)
        var = jnp.mean(jnp.square(y - mu), axis=-1, keepdims=True)
        y = (y - mu) * lax.rsqrt(var + NORM_EPS) * lg_ref[...] + lb_ref[...]
        y = y * _sigmoid(y)
        yc_ref[0, r0:r0 + rows, :] = _dot(y.astype(BF16), cpw_ref[...])

    ptail = pbuf[tt:tt + POOL_PAD, :]
    ctail = cbuf[tt:tt + CONV_PAD, :]
    pnew_ref[0] = ptail
    cnew_ref[0] = ctail
    pbuf[0:POOL_PAD, :] = ptail
    cbuf[0:CONV_PAD, :] = ctail


def _poolconv(up, cu, pool_prev, conv_prev, q_pos0, pool_w, pool_scale, conv_w, conv_b, ln_g, ln_b, conv_pw):
    b, t_len, _ = up.shape
    tt = 512 if t_len % 512 == 0 else t_len
    rows = 64 if tt % 64 == 0 else tt
    pprev = jnp.pad(pool_prev, ((0, 0), (POOL_PAD - pool_prev.shape[1], 0), (0, 0)))
    cprev = jnp.pad(conv_prev, ((0, 0), (CONV_PAD - conv_prev.shape[1], 0), (0, 0)))
    pw_bd = jax.scipy.linalg.block_diag(*[pool_w[g] for g in range(pool_w.shape[0])]).astype(BF16)
    cw_pad = jnp.pad(conv_w, ((0, CONV_PAD - CONV_W), (0, 0)))
    seq = lambda i, j: (i, j, 0)
    per_b = lambda i, j: (i, 0, 0)
    const = lambda i, j: (0, 0)
    row = pl.BlockSpec((1, GROUP_W), const)
    ya, yc, pnew, cnew = pl.pallas_call(
        functools.partial(_poolconv_kernel, tt=tt, q_pos0=q_pos0, rows=rows),
        grid=(b, t_len // tt),
        in_specs=[pl.BlockSpec((1, tt, GROUP_W), seq), pl.BlockSpec((1, tt, GROUP_W), seq),
                  pl.BlockSpec((1, POOL_PAD, GROUP_W), per_b), pl.BlockSpec((1, CONV_PAD, GROUP_W), per_b),
                  pl.BlockSpec((GROUP_W, GROUP_W), const), row,
                  pl.BlockSpec((CONV_PAD, GROUP_W), const), row, row, row,
                  pl.BlockSpec((GROUP_W, GROUP_W), const)],
        out_specs=[pl.BlockSpec((1, tt, GROUP_W), seq), pl.BlockSpec((1, tt, GROUP_W), seq),
                   pl.BlockSpec((1, POOL_PAD, GROUP_W), per_b), pl.BlockSpec((1, CONV_PAD, GROUP_W), per_b)],
        out_shape=[jax.ShapeDtypeStruct((b, t_len, GROUP_W), F32), jax.ShapeDtypeStruct((b, t_len, GROUP_W), F32),
                   jax.ShapeDtypeStruct((b, POOL_PAD, GROUP_W), F32), jax.ShapeDtypeStruct((b, CONV_PAD, GROUP_W), F32)],
        scratch_shapes=[pltpu.VMEM((POOL_PAD + tt, GROUP_W), F32), pltpu.VMEM((CONV_PAD + tt + 8, GROUP_W), F32),
                        pltpu.VMEM((8, rows + 8, GROUP_W), F32)],
        compiler_params=_params(("parallel", "arbitrary")),
        name="poolconv",
    )(up, cu, pprev, cprev, pw_bd, pool_scale.reshape(1, GROUP_W), cw_pad, conv_b.reshape(1, GROUP_W),
      ln_g.reshape(1, GROUP_W), ln_b.reshape(1, GROUP_W), conv_pw.astype(BF16))
    return ya, yc, pnew[:, POOL_PAD - pool_prev.shape[1]:], cnew[:, CONV_PAD - conv_prev.shape[1]:]


def _stack_heads(q):
    head = _lane_head(q.shape)
    return jnp.concatenate([jnp.where(head == h, q, 0.0) for h in range(N_HEADS)], axis=0)


def _select_blocks(gate, n_past):
    lane = lax.broadcasted_iota(jnp.int32, gate.shape, 1)
    past = lane < n_past
    g = jnp.where(past, gate, NEG_BIG)
    sel = jnp.zeros(gate.shape, jnp.bool_)
    for _ in range(MOBA_TOPK):
        mx = jnp.max(g, axis=1, keepdims=True)
        idx = jnp.min(jnp.where(g == mx, lane, 1 << 20), axis=1, keepdims=True)
        pick = lane == idx
        sel = jnp.logical_or(sel, pick)
        g = jnp.where(pick, -jnp.inf, g)
    return jnp.logical_and(sel, past)


def _select_block_rows(gate_t, n_past, nblk):
    n = lax.broadcasted_iota(jnp.int32, gate_t.shape, 0)
    past = n < n_past
    g = jnp.where(past, gate_t, jnp.where(n < nblk, NEG_BIG, -jnp.inf))
    sel = jnp.zeros(gate_t.shape, jnp.bool_)
    for _ in range(MOBA_TOPK):
        mx = jnp.max(g, axis=0, keepdims=True)
        idx = jnp.min(jnp.where(g == mx, n, 1 << 20), axis=0, keepdims=True)
        pick = n == idx
        sel = jnp.logical_or(sel, pick)
        g = jnp.where(pick, -jnp.inf, g)
    return jnp.logical_and(sel, past)


def _moba_prefill_kernel(qp_ref, ka_ref, va_ref, km_ref, o_ref,
                         kmr_ref, qa_ref, s_ref, mx_ref, m_ref, acc_ref, *, nblk):
    i = pl.program_id(1)
    blk = MOBA_BLOCK
    kb = kmr_ref.shape[1]
    heads = [slice(h * 128, (h + 1) * 128) for h in range(N_HEADS)]

    @pl.when(i == 0)
    def _():
        kmr_ref[...] = jnp.zeros_like(kmr_ref)
        for h in range(N_HEADS):
            for n in range(nblk):
                kmr_ref[h, n:n + 1, :] = km_ref[0, n][:, heads[h]]

    n_row = lax.broadcasted_iota(jnp.int32, (kb, blk), 0)
    for h in range(N_HEADS):
        qh = qp_ref[0, :, heads[h]]
        gate_t = _dot_nt(kmr_ref[h], qh, precision=HI)
        sel = _select_block_rows(gate_t, i, nblk)
        bias_t = jnp.where(jnp.logical_or(sel, n_row == i), 0.0, NEG_BIG)
        bias_t = jnp.concatenate([jnp.zeros((HEAD_DIM, blk), F32), bias_t,
                                  jnp.zeros((128 - HEAD_DIM - kb, blk), F32)], axis=0)
        qa_ref[h] = (qh * (ATTN_SCALE * LOG2_E) + bias_t.T).astype(BF16)
        mx_ref[h] = jnp.full((blk, 128), NEG_BIG, F32)
        acc_ref[h] = jnp.zeros((blk, 128), F32)

    row = lax.broadcasted_iota(jnp.int32, (blk, blk), 0)
    col = lax.broadcasted_iota(jnp.int32, (blk, blk), 1)
    n_pairs = (i + 2) // 2

    def scores(jj, carry):
        for u in range(2):
            j = 2 * jj + u
            j0 = pl.multiple_of(j * blk, blk)
            allow = col + (j - i) * blk <= row
            for h in range(N_HEADS):
                s = jnp.where(allow, _dot_nt(qa_ref[h], ka_ref[0, pl.ds(j0, blk), heads[h]]), NEG_BIG)
                s_ref[h, j] = s
                mx_ref[h] = jnp.maximum(mx_ref[h], jnp.maximum(s[:, :128], s[:, 128:]))
        return carry

    lax.fori_loop(0, n_pairs, scores, 0)
    for h in range(N_HEADS):
        m_ref[h] = jnp.broadcast_to(jnp.max(mx_ref[h], axis=1, keepdims=True), (blk, 128))

    def weighted(jj, carry):
        for u in range(2):
            j = 2 * jj + u
            j0 = pl.multiple_of(j * blk, blk)
            for h in range(N_HEADS):
                s = s_ref[h, j]
                m = m_ref[h]
                p = jnp.concatenate([jnp.exp2(s[:, :128] - m), jnp.exp2(s[:, 128:] - m)], axis=1).astype(BF16)
                acc_ref[h] += _dot(p, va_ref[0, pl.ds(j0, blk), heads[h]])
        return carry

    lax.fori_loop(0, n_pairs, weighted, 0)
    lane = lax.broadcasted_iota(jnp.int32, (blk, 128), 1)
    for h in range(N_HEADS):
        acc = acc_ref[h]
        o_ref[0, :, heads[h]] = jnp.where(lane < HEAD_DIM, acc / acc[:, HEAD_DIM:HEAD_DIM + 1], 0.0)


def _moba_prefill(qp, ka, va, kmean):
    b, t_len, wide = qp.shape
    nblk = t_len // MOBA_BLOCK
    assert nblk % 2 == 0 and nblk <= 128 - HEAD_DIM
    kb = -(-nblk // 8) * 8
    blk = MOBA_BLOCK
    per_b = lambda i, j: (i, 0, 0)
    return pl.pallas_call(
        functools.partial(_moba_prefill_kernel, nblk=nblk),
        grid=(b, nblk),
        in_specs=[pl.BlockSpec((1, blk, wide), lambda i, j: (i, j, 0)),
                  pl.BlockSpec((1, t_len, wide), per_b, pipeline_mode=pl.Buffered(1)),
                  pl.BlockSpec((1, t_len, wide), per_b, pipeline_mode=pl.Buffered(1)),
                  pl.BlockSpec((1, nblk, 1, wide), lambda i, j: (i, 0, 0, 0))],
        out_specs=pl.BlockSpec((1, blk, wide), lambda i, j: (i, j, 0)),
        out_shape=jax.ShapeDtypeStruct((b, t_len, wide), F32),
        scratch_shapes=[pltpu.VMEM((N_HEADS, kb, 128), F32), pltpu.VMEM((N_HEADS, blk, 128), BF16),
                        pltpu.VMEM((N_HEADS, nblk, blk, blk), F32), pltpu.VMEM((N_HEADS, blk, 128), F32),
                        pltpu.VMEM((N_HEADS, blk, 128), F32), pltpu.VMEM((N_HEADS, blk, 128), F32)],
        compiler_params=_params(("parallel", "arbitrary")),
        name="moba_prefill",
    )(qp, ka, va, kmean)


def _moba_decode_kernel(pt_ref, q_ref, kn_ref, vn_ref, ck_ref, cv_ref, o_ref,
                        buf, sem, s_ref, p_ref, own_ref, *, layer, n_pages, tq):
    b = pl.program_id(0)
    ch = PAGES_PER_CHUNK
    n_chunks = n_pages // ch
    page = buf.shape[3]
    pages_per_blk = MOBA_BLOCK // page
    n_past = n_pages // pages_per_blk
    rows = N_HEADS * tq

    def page_copy(cache_ref, c, pg, slot, seq=b):
        return pltpu.make_async_copy(cache_ref.at[layer, pt_ref[seq, c * ch + pg]], buf.at[slot, pg], sem.at[slot])

    def start_chunk(cache_ref, c, slot, seq=b):
        for pg in range(ch):
            page_copy(cache_ref, c, pg, slot, seq).start()

    def wait_chunk(cache_ref, c, slot):
        for pg in range(ch):
            page_copy(cache_ref, c, pg, slot).wait()

    q4 = _stack_heads(q_ref[0]) * ATTN_SCALE
    q_hi = q4.astype(BF16)
    qq = jnp.concatenate([q_hi, (q4 - q_hi.astype(F32)).astype(BF16)], axis=0)

    def both(s2):
        return s2[:rows] + s2[rows:]

    @pl.when(b == 0)
    def _():
        start_chunk(ck_ref, 0, 0)

    def k_chunk(c, carry):
        slot = c % 2

        @pl.when(c + 1 < n_chunks)
        def _():
            start_chunk(ck_ref, c + 1, 1 - slot)

        @pl.when(c + 1 == n_chunks)
        def _():
            start_chunk(cv_ref, 0, 1 - slot)

        wait_chunk(ck_ref, c, slot)
        for pg in range(ch):
            s_ref[c * ch + pg] = both(_dot(qq, buf[slot, pg].astype(BF16)))
        return carry

    lax.fori_loop(0, n_chunks, k_chunk, 0)

    lane = lax.broadcasted_iota(jnp.int32, (rows, 128), 1)
    gate = jnp.zeros((rows, 128), F32)
    for n in range(n_past):
        tot = s_ref[n * pages_per_blk]
        for pg in range(1, pages_per_blk):
            tot = tot + s_ref[n * pages_per_blk + pg]
        gate = jnp.where(lane == n, jnp.sum(tot, axis=1, keepdims=True), gate)
    gate = gate * (1.0 / (MOBA_BLOCK * ATTN_SCALE))
    sel = _select_blocks(gate, n_past).astype(F32)

    own_ref[...] = jnp.zeros_like(own_ref)
    own_ref[0, 0:tq, :] = kn_ref[0]
    own_ref[1, 0:tq, :] = vn_ref[0]
    s_own = both(_dot_nt(qq, own_ref[0].astype(BF16)))
    s_own = jnp.where(lane <= lax.broadcasted_iota(jnp.int32, (rows, 128), 0) % tq, s_own, NEG_BIG)

    mx = s_own
    for n in range(n_past):
        ok = sel[:, n:n + 1] > 0.5
        for pg in range(n * pages_per_blk, (n + 1) * pages_per_blk):
            s_pg = jnp.where(ok, s_ref[pg], NEG_BIG)
            s_ref[pg] = s_pg
            mx = jnp.maximum(mx, s_pg)
    m = jnp.max(mx, axis=1, keepdims=True)
    p_own = jnp.exp(s_own - m)
    lsum = p_own
    for pg in range(n_pages):
        p = jnp.exp(s_ref[pg] - m)
        lsum = lsum + p
        p_ref[pg] = p.astype(BF16)
    l = jnp.sum(lsum, axis=1, keepdims=True)

    def v_chunk(c, acc):
        slot = (n_chunks + c) % 2

        @pl.when(c + 1 < n_chunks)
        def _():
            start_chunk(cv_ref, c + 1, 1 - slot)

        @pl.when(jnp.logical_and(c + 1 == n_chunks, b + 1 < pl.num_programs(0)))
        def _():
            start_chunk(ck_ref, 0, 1 - slot, seq=b + 1)

        wait_chunk(cv_ref, c, slot)
        for pg in range(ch):
            acc = acc + _dot_nt(p_ref[c * ch + pg], buf[slot, pg].astype(BF16))
        return acc

    acc = lax.fori_loop(0, n_chunks, v_chunk, _dot(p_own.astype(BF16), own_ref[1].astype(BF16)))
    acc = acc / l
    head = _lane_head((tq, GROUP_W))
    out = jnp.zeros((tq, GROUP_W), F32)
    for h in range(N_HEADS):
        out = out + jnp.where(head == h, acc[h * tq:(h + 1) * tq, :], 0.0)
    o_ref[0] = out


def _moba_decode(q, k_new, v_new, cache_k, cache_v, page_table, layer):
    b, tq, _ = q.shape
    n_pages = page_table.shape[1]
    page = cache_k.shape[3]
    assert page == 128 and MOBA_BLOCK % page == 0 and n_pages % (MOBA_BLOCK // page) == 0
    assert n_pages % PAGES_PER_CHUNK == 0 and tq <= 128 and n_pages * page // MOBA_BLOCK <= 128
    rows = N_HEADS * tq
    per_b = lambda i, pt: (i, 0, 0)
    grid_spec = pltpu.PrefetchScalarGridSpec(
        num_scalar_prefetch=1,
        grid=(b,),
        in_specs=[pl.BlockSpec((1, tq, GROUP_W), per_b), pl.BlockSpec((1, tq, GROUP_W), per_b),
                  pl.BlockSpec((1, tq, GROUP_W), per_b),
                  pl.BlockSpec(memory_space=pl.ANY), pl.BlockSpec(memory_space=pl.ANY)],
        out_specs=pl.BlockSpec((1, tq, GROUP_W), per_b),
        scratch_shapes=[pltpu.VMEM((2, PAGES_PER_CHUNK, GROUP_W, page), F32),
                        pltpu.SemaphoreType.DMA((2,)),
                        pltpu.VMEM((n_pages, rows, page), F32), pltpu.VMEM((n_pages, rows, page), BF16),
                        pltpu.VMEM((2, 128, GROUP_W), F32)],
    )
    return pl.pallas_call(
        functools.partial(_moba_decode_kernel, layer=layer, n_pages=n_pages, tq=tq),
        grid_spec=grid_spec,
        out_shape=jax.ShapeDtypeStruct((b, tq, GROUP_W), F32),
        compiler_params=_params(("arbitrary",)),
        name="moba_decode",
    )(page_table, q, k_new, v_new, cache_k, cache_v)


def _hgrn_kernel(x_ref, st0_ref, lb_ref, ng_ref, y_ref, stout_ref, st_ref, qst_ref, upd_ref, dec_ref,
                 *, tt, chunk, sub):
    t = pl.program_id(1)
    n_sub = chunk // sub

    bd = _head_ones()
    spread = (lax.broadcasted_iota(jnp.int32, (GROUP_W, HEAD_DIM), 0) % HEAD_DIM
              == lax.broadcasted_iota(jnp.int32, (GROUP_W, HEAD_DIM), 1)).astype(F32)

    @pl.when(t == 0)
    def _():
        st_ref[...] = jnp.where(bd, _dot_nt(spread, st0_ref[0], precision=HI), 0.0)

    lb = lb_ref[...]
    head = _lane_head((1, GROUP_W))
    hmask = [head == h for h in range(N_HEADS)]
    ones_b = bd.astype(BF16)
    ones_f = bd.astype(F32)
    tri = (lax.broadcasted_iota(jnp.int32, (chunk, chunk), 1)
           <= lax.broadcasted_iota(jnp.int32, (chunk, chunk), 0)).astype(F32)
    t_ge = [lax.broadcasted_iota(jnp.int32, (sub, GROUP_W), 0) >= s for s in range(sub)]

    n_chunks = tt // chunk
    if n_sub > 1:
        n_used = sub * (n_sub * (n_sub - 1) // 2)
        n_col = -(-n_used // 128) * 128
        col = lax.broadcasted_iota(jnp.int32, (N_HEADS * chunk, n_col), 1)
        row_sub = (lax.broadcasted_iota(jnp.int32, (N_HEADS * chunk, n_col), 0) % chunk) // sub
        keep = jnp.zeros((N_HEADS * chunk, n_col), jnp.bool_)
        c0 = 0
        for i in range(1, n_sub):
            keep = jnp.logical_or(keep, jnp.logical_and(row_sub == i, jnp.logical_and(col >= c0, col < c0 + i * sub)))
            c0 += i * sub

    def local_part(ci):
        r0 = ci * chunk if isinstance(ci, int) else pl.multiple_of(ci * chunk, chunk)
        hq = x_ref[0, pl.ds(r0, chunk), 0 * GROUP_W:1 * GROUP_W]
        hf = x_ref[0, pl.ds(r0, chunk), 1 * GROUP_W:2 * GROUP_W]
        hi = x_ref[0, pl.ds(r0, chunk), 2 * GROUP_W:3 * GROUP_W]
        f = lb + (1.0 - lb) * _sigmoid(hf)
        kin = 1.0 - f
        bc = jnp.dot(tri, jnp.log2(f), precision=HI, preferred_element_type=F32)
        b_last = bc[chunk - 1:chunk, :]
        qst_ref[pl.ds(r0, chunk), :] = hq * jnp.exp2(bc)
        k_state = (kin * jnp.exp2(b_last - bc)).astype(BF16)
        upd_ref[ci] = jnp.where(bd, _dot(hi.T.astype(BF16), k_state), 0.0)
        dec_ref[ci] = jnp.exp2(b_last)

        o = None
        if n_sub > 1:
            bq = jnp.concatenate(
                [jnp.zeros((sub, GROUP_W), F32)]
                + [jnp.broadcast_to(bc[i * sub - 1:i * sub, :], (sub, GROUP_W)) for i in range(1, n_sub)], axis=0)
            q_off = hq * jnp.exp2(bc - bq)
            q4 = jnp.concatenate([jnp.where(hmask[h], q_off, 0.0) for h in range(N_HEADS)], axis=0).astype(BF16)
            pad = [jnp.zeros((n_col - n_used, GROUP_W), F32)] if n_col > n_used else []
            k_st = jnp.concatenate(
                [kin[0:i * sub, :] * jnp.exp2(bc[i * sub - 1:i * sub, :] - bc[0:i * sub, :])
                 for i in range(1, n_sub)] + pad, axis=0).astype(BF16)
            v_st = jnp.concatenate([hi[0:i * sub, :] for i in range(1, n_sub)] + pad, axis=0).astype(BF16)
            att = jnp.where(keep, _dot_nt(q4, k_st), 0.0).astype(BF16)
            res = _dot(att, v_st)
            o = jnp.where(hmask[0], res[0:chunk], 0.0)
            for h in range(1, N_HEADS):
                o = o + jnp.where(hmask[h], res[h * chunk:(h + 1) * chunk], 0.0)

        d_rows = []
        for i in range(n_sub):
            rs = slice(i * sub, (i + 1) * sub)
            q_i, bc_i, k_i = hq[rs], bc[rs], kin[rs]
            for s in range(sub):
                e = jnp.exp2(jnp.where(t_ge[s], bc_i - bc_i[s:s + 1, :], NEG_BIG))
                d_rows.append(q_i * e * k_i[s:s + 1, :])
        a = _dot(jnp.concatenate(d_rows, axis=0).astype(BF16), ones_b)
        o_diag = []
        for i in range(n_sub):
            v_i = hi[i * sub:(i + 1) * sub]
            acc = a[i * sub * sub:i * sub * sub + sub, :] * v_i[0:1, :]
            for s in range(1, sub):
                base = (i * sub + s) * sub
                acc = acc + a[base:base + sub, :] * v_i[s:s + 1, :]
            o_diag.append(acc)
        o_diag = jnp.concatenate(o_diag, axis=0) if n_sub > 1 else o_diag[0]
        y_ref[0, pl.ds(r0, chunk), :] = o_diag if o is None else o + o_diag

    def carried_part(ci, carry):
        r0 = pl.multiple_of(ci * chunk, chunk)
        st = st_ref[...]
        y_ref[0, pl.ds(r0, chunk), :] += _dot_nt(qst_ref[pl.ds(r0, chunk), :].astype(BF16), st.astype(BF16))
        st_ref[...] = st * dec_ref[ci] + upd_ref[ci]
        return carry

    if n_chunks % 2 == 0:
        def pair(cp, carry):
            local_part(2 * cp)
            local_part(2 * cp + 1)
            return carry
        lax.fori_loop(0, n_chunks // 2, pair, 0)
    else:
        for ci in range(n_chunks):
            local_part(ci)
    lax.fori_loop(0, n_chunks, carried_part, 0)

    o = y_ref[0]
    hg = x_ref[0, :, 3 * GROUP_W:4 * GROUP_W]
    ms = jnp.dot(o * o, ones_f, precision=HI, preferred_element_type=F32) * (1.0 / HEAD_DIM)
    y_ref[0] = o * lax.rsqrt(ms + NORM_EPS) * ng_ref[...] * (hg * _sigmoid(hg))

    @pl.when(t == pl.num_programs(1) - 1)
    def _():
        stout_ref[0] = jnp.dot(st_ref[...].T, spread, precision=HI, preferred_element_type=F32)


def _hgrn(x4, state, lb, norm_g):
    b, t_len, _ = x4.shape
    chunk = HGRN_CHUNK if t_len % HGRN_CHUNK == 0 else t_len
    sub = HGRN_SUB if chunk % HGRN_SUB == 0 else chunk
    tt = 512 if t_len % 512 == 0 else (chunk if t_len == chunk else t_len)
    st_rows = pl.BlockSpec((1, GROUP_W, HEAD_DIM), lambda i, j: (i, 0, 0))
    y, st = pl.pallas_call(
        functools.partial(_hgrn_kernel, tt=tt, chunk=chunk, sub=sub),
        grid=(b, t_len // tt),
        in_specs=[pl.BlockSpec((1, tt, 4 * GROUP_W), lambda i, j: (i, j, 0)), st_rows,
                  pl.BlockSpec((1, GROUP_W), lambda i, j: (0, 0)),
                  pl.BlockSpec((1, GROUP_W), lambda i, j: (0, 0))],
        out_specs=[pl.BlockSpec((1, tt, GROUP_W), lambda i, j: (i, j, 0)), st_rows],
        out_shape=[jax.ShapeDtypeStruct((b, t_len, GROUP_W), F32),
                   jax.ShapeDtypeStruct((b, GROUP_W, HEAD_DIM), F32)],
        scratch_shapes=[pltpu.VMEM((GROUP_W, GROUP_W), F32), pltpu.VMEM((tt, GROUP_W), F32),
                        pltpu.VMEM((tt // chunk, GROUP_W, GROUP_W), F32), pltpu.VMEM((tt // chunk, 1, GROUP_W), F32)],
        compiler_params=_params(("parallel", "arbitrary")),
        name="hgrn",
    )(x4, state.reshape(b, GROUP_W, HEAD_DIM), lb.reshape(1, GROUP_W), jnp.tile(norm_g, N_HEADS).reshape(1, GROUP_W))
    return y, st.reshape(state.shape)


def _trunk(x, q_pos0, pool_st, conv_st, hgrn_st, cache_k, cache_v, page_table, w):
    b, t_len, d = x.shape
    n = b * t_len
    depth = w["ln_ffn1"].shape[0]
    lbs = jax.nn.softmax(w["hgrn_lower_bounds"].astype(F32), axis=0)
    lbs = jnp.cumsum(lbs, axis=0) - lbs[0]
    prefill = cache_k is None
    xf = x.reshape(n, d)
    ks, vs, pools, convs, hgrns = [], [], [], [], []
    kv_stack = None
    for l in range(depth):
        xf = _ffn(xf, w["ln_ffn1"][l], w["ffn1_w_gate"][l], w["ffn1_w_up"][l], w["ffn1_w_down"][l])
        outs = _inproj(xf, t_len, q_pos0, w["ln_mix"][l], w["w_in"][l], w["q_norm"][l], w["k_norm"][l], prefill,
                       kv_prev=kv_stack)
        seq = lambda a: a.reshape(b, t_len, a.shape[-1])
        wo = w["w_out"][l]
        wo_parts = [wo[g * GROUP_W:(g + 1) * GROUP_W] for g in range(4)]
        if prefill:
            up, kt, vt, qp, ka, va, cu, x4, kmean = outs
            kv_stack = (kt, vt)
            yb =_moba_prefill(seq(qp), seq(ka), seq(va), kmean.reshape(b, t_len // MOBA_BLOCK, 1, 2 * GROUP_W))
            wo_parts[1] = jnp.pad(wo_parts[1].reshape(N_HEADS, HEAD_DIM, d),
                                  ((0, 0), (0, 128 - HEAD_DIM), (0, 0))).reshape(N_HEADS * 128, d)
        else:
            up, q, k, v, cu, x4 = outs
            yb = _moba_decode(seq(q), seq(k), seq(v), cache_k, cache_v, page_table, l)
            ks.append(k.reshape(b, t_len, N_HEADS, HEAD_DIM))
            vs.append(v.reshape(b, t_len, N_HEADS, HEAD_DIM))
        ya, yc, pool_new, conv_new = _poolconv(
            seq(up), seq(cu), pool_st[l], conv_st[l], q_pos0, w["pool_w"][l], w["pool_scale"][l],
            w["conv_w"][l], w["conv_b"][l], w["conv_ln_g"][l], w["conv_ln_b"][l], w["conv_pw"][l])
        yd, s_new = _hgrn(seq(x4), hgrn_st[l], lbs[l], w["hgrn_norm"][l])
        flat = lambda a: a.reshape(n, a.shape[-1])
        xf = _ffn(xf, w["ln_ffn2"][l], w["ffn2_w_gate"][l], w["ffn2_w_up"][l], w["ffn2_w_down"][l],
                  mix=tuple(zip((flat(ya), flat(yb), flat(yc), flat(yd)), wo_parts)))
        pools.append(pool_new)
        convs.append(conv_new)
        hgrns.append(s_new)
    if prefill:
        k_all, v_all = (jnp.transpose(a.reshape(depth, b, N_HEADS, HEAD_DIM, t_len), (0, 1, 4, 2, 3)) for a in kv_stack)
    else:
        k_all, v_all = jnp.stack(ks), jnp.stack(vs)
    return (xf.reshape(b, t_len, d), k_all, v_all, jnp.stack(pools), jnp.stack(convs), jnp.stack(hgrns))


def kernel(x_prompt, x_sample, cache_k, cache_v, page_table, state_pool, state_conv, state_hgrn, ln_ffn1, ffn1_w_gate, ffn1_w_up, ffn1_w_down, ln_mix, w_in, w_out, pool_w, pool_scale, q_norm, k_norm, conv_w, conv_b, conv_ln_g, conv_ln_b, conv_pw, hgrn_lower_bounds, hgrn_norm, ln_ffn2, ffn2_w_gate, ffn2_w_up, ffn2_w_down):
    w = dict(ln_ffn1=ln_ffn1, ln_mix=ln_mix, ln_ffn2=ln_ffn2, pool_w=pool_w, pool_scale=pool_scale,
             q_norm=q_norm, k_norm=k_norm, conv_w=conv_w, conv_b=conv_b, conv_ln_g=conv_ln_g,
             conv_ln_b=conv_ln_b, conv_pw=conv_pw, hgrn_lower_bounds=hgrn_lower_bounds, hgrn_norm=hgrn_norm)
    for name, arr in (("ffn1_w_gate", ffn1_w_gate), ("ffn1_w_up", ffn1_w_up), ("ffn1_w_down", ffn1_w_down),
                      ("w_in", w_in), ("w_out", w_out), ("ffn2_w_gate", ffn2_w_gate), ("ffn2_w_up", ffn2_w_up),
                      ("ffn2_w_down", ffn2_w_down)):
        w[name] = arr.astype(BF16)
    depth = ln_ffn1.shape[0]
    bp = x_prompt.shape[0]
    dt = x_prompt.dtype
    zero_pool = jnp.zeros((depth, bp) + state_pool.shape[2:], dt)
    zero_conv = jnp.zeros((depth, bp) + state_conv.shape[2:], dt)
    zero_hgrn = jnp.zeros((depth, bp) + state_hgrn.shape[2:], dt)
    y_p, k_p, v_p, pool_p, conv_p, hgrn_p = _trunk(
        x_prompt, 0, zero_pool, zero_conv, zero_hgrn, None, None, None, w)
    past_len = page_table.shape[1] * cache_k.shape[2]
    to_pages = lambda c: jnp.transpose(c, (0, 1, 3, 4, 2)).reshape(c.shape[:2] + (GROUP_W, c.shape[2]))
    ck, cv = to_pages(cache_k), to_pages(cache_v)
    y_s, k_s, v_s, pool_s, conv_s, hgrn_s = _trunk(
        x_sample, past_len, state_pool, state_conv, state_hgrn, ck, cv, page_table, w)
    return (y_p, y_s, k_p, v_p, k_s, v_s, pool_p, pool_s, conv_p, conv_s, hgrn_p, hgrn_s)
```

```python
import functools

import jax
import jax.numpy as jnp
from jax import lax
from jax.experimental import pallas as pl
from jax.experimental.pallas import tpu as pltpu

F32 = jnp.float32
BF16 = jnp.bfloat16
HI = lax.Precision.HIGHEST

NORM_EPS = 1e-6
GROUP_W = 256
N_HEADS = 4
HEAD_DIM = GROUP_W // N_HEADS
N_IN_SPLITS = 10
POOL_WINDOWS = (2, 4, 8, 16)
POOL_PAD = 16
CONV_W = 31
CONV_PAD = 32
CONV_LEAD = CONV_PAD - (CONV_W - 1)
MOBA_BLOCK = 256
MOBA_TOPK = 3
ROPE_DIM = HEAD_DIM // 4
ROPE_THETA = 500000.0
ATTN_SCALE = HEAD_DIM ** -0.5
LOG2_E = 1.4426950408889634
NEG_BIG = -1e30
HGRN_CHUNK = 64
HGRN_SUB = 8
PAGES_PER_CHUNK = 128
MXU_TILE = 256
VMEM_LIMIT = 56 * 1024 * 1024


def _params(sem=None):
    return pltpu.CompilerParams(dimension_semantics=sem, vmem_limit_bytes=VMEM_LIMIT)


def _dot(a, b):
    return jnp.dot(a, b, preferred_element_type=F32)


def _dot_nt(a, b, precision=None):
    return lax.dot_general(a, b, (((1,), (1,)), ((), ())), precision=precision,
                           preferred_element_type=F32)


def _lane_head(shape):
    return lax.broadcasted_iota(jnp.int32, shape, len(shape) - 1) // HEAD_DIM


def _head_ones():
    r = lax.broadcasted_iota(jnp.int32, (GROUP_W, GROUP_W), 0) // HEAD_DIM
    c = lax.broadcasted_iota(jnp.int32, (GROUP_W, GROUP_W), 1) // HEAD_DIM
    return r == c


def _sigmoid(x):
    return 1.0 / (1.0 + jnp.exp(-x))


def _ffn_kernel(*refs, n_mix, ff_chunk):
    x_ref = refs[0]
    mix_refs = refs[1:1 + 2 * n_mix]
    g_ref, wg_ref, wu_ref, wd_ref, o_ref = refs[1 + 2 * n_mix:]
    x = x_ref[...]
    for idx in range(n_mix):
        x = x + _dot(mix_refs[idx][...].astype(BF16), mix_refs[n_mix + idx][...])
    ms = jnp.mean(x * x, axis=-1, keepdims=True)
    h = (x * lax.rsqrt(ms + NORM_EPS) * g_ref[...]).astype(BF16)
    acc = None
    for c0, c1 in zip((0,) + ff_chunk, ff_chunk + (wg_ref.shape[1],)):
        g = _dot(h, wg_ref[:, c0:c1])
        u = _dot(h, wu_ref[:, c0:c1])
        a = (g * _sigmoid(g) * u).astype(BF16)
        part = _dot(a, wd_ref[c0:c1, :])
        acc = part if acc is None else acc + part
    o_ref[...] = x + 0.5 * acc


def _resident(shape):
    return pl.BlockSpec(shape, lambda i: (0,) * len(shape), pipeline_mode=pl.Buffered(1))


def _ffn(x, gain, wg, wu, wd, mix=()):
    n, d = x.shape
    d_ff = wg.shape[1]
    tm = 512 if n % 512 == 0 else n
    ff_chunk = (-(-(d_ff // MXU_TILE) // 2) * MXU_TILE,) if d_ff % MXU_TILE == 0 and d_ff > MXU_TILE else ()
    tok = lambda i: (i, 0)
    in_specs = [pl.BlockSpec((tm, d), tok)]
    in_specs += [pl.BlockSpec((tm, y.shape[1]), tok) for y, _ in mix]
    in_specs += [_resident(w.shape) for _, w in mix]
    in_specs += [_resident((1, d)), _resident(wg.shape), _resident(wu.shape), _resident(wd.shape)]
    args = [x] + [y for y, _ in mix] + [w for _, w in mix] + [gain.reshape(1, d), wg, wu, wd]
    return pl.pallas_call(
        functools.partial(_ffn_kernel, n_mix=len(mix), ff_chunk=ff_chunk),
        grid=(n // tm,),
        in_specs=in_specs,
        out_specs=pl.BlockSpec((tm, d), tok),
        out_shape=jax.ShapeDtypeStruct((n, d), F32),
        compiler_params=_params(("parallel",)),
        name="ffn_merge" if mix else "ffn",
    )(*args)


def _spread_heads(x, fill):
    lane = lax.broadcasted_iota(jnp.int32, (x.shape[0], 128), 1)
    odd = pltpu.roll(x, GROUP_W - HEAD_DIM, axis=1)
    parts = (x[:, :128], odd[:, :128], x[:, 128:], odd[:, 128:])
    return jnp.concatenate([jnp.where(lane < HEAD_DIM, p, fill) for p in parts], axis=1)


def _inproj_kernel(*refs, prefill, tm, per_seq, n_prev):
    x_ref, g_ref, w_ref, qg_ref, kg_ref, cos_ref, sin_ref = refs[:7]
    if prefill:
        ktp_ref, vtp_ref = refs[7:9] if n_prev else (None, None)
        up_ref, kt_ref, vt_ref, qp_ref, ka_ref, va_ref, cu_ref, hg_ref, km_ref = refs[7 + (2 if n_prev else 0):]
    else:
        up_ref, q_ref, k_ref, v_ref, cu_ref, hg_ref = refs[7:]
    x = x_ref[...]
    ms = jnp.mean(x * x, axis=-1, keepdims=True)
    h = (x * lax.rsqrt(ms + NORM_EPS) * g_ref[...]).astype(BF16)

    def proj(c):
        return _dot(h, w_ref[:, c * GROUP_W:(c + 1) * GROUP_W])

    ones = _head_ones().astype(F32)
    cos = cos_ref[...]
    sin = sin_ref[...]
    d = lax.broadcasted_iota(jnp.int32, cos.shape, 1) % HEAD_DIM
    half = ROPE_DIM // 2

    def norm_rope(p, gain):
        ms_h = jnp.dot(p * p, ones, precision=HI, preferred_element_type=F32) * (1.0 / HEAD_DIM)
        y = p * lax.rsqrt(ms_h + NORM_EPS) * gain
        partner = jnp.where(d < half, pltpu.roll(y, GROUP_W - half, axis=1), pltpu.roll(y, half, axis=1))
        return y * cos + partner * sin

    up_ref[...] = proj(0)
    q = norm_rope(proj(1), qg_ref[...])
    k = norm_rope(proj(2), kg_ref[...])
    v = proj(3)
    cu_ref[...] = proj(4) * _sigmoid(proj(5))
    for c in range(4):
        hg_ref[:, c * GROUP_W:(c + 1) * GROUP_W] = proj(6 + c)
    if not prefill:
        q_ref[...] = q
        k_ref[...] = k
        v_ref[...] = v
        return
    if n_prev:
        kt_ref[0:n_prev] = ktp_ref[...]
        vt_ref[0:n_prev] = vtp_ref[...]
    kt_ref[n_prev, 0] = k.T
    vt_ref[n_prev, 0] = v.T
    lane = lax.broadcasted_iota(jnp.int32, (tm, 128), 1)
    row = lax.broadcasted_iota(jnp.int32, (tm, 128), 0)
    blk = ((pl.program_id(0) % per_seq) * tm + row) // MOBA_BLOCK
    zero = jnp.zeros((tm, 128), F32)
    kp = _spread_heads(k, zero)
    qp_ref[...] = _spread_heads(q, zero)
    ka_ref[...] = _spread_heads(k, jnp.where(lane - HEAD_DIM == blk, 1.0, 0.0)).astype(BF16)
    va_ref[...] = _spread_heads(v, jnp.where(lane == HEAD_DIM, 1.0, 0.0)).astype(BF16)
    for n in range(km_ref.shape[0]):
        km_ref[n] = jnp.mean(kp[n * MOBA_BLOCK:(n + 1) * MOBA_BLOCK, :], axis=0, keepdims=True)


def _rope_tables(t_len, q_pos0):
    half = ROPE_DIM // 2
    inv = ROPE_THETA ** (-jnp.arange(half, dtype=F32) * 2.0 / ROPE_DIM)
    pos = q_pos0 + jnp.arange(t_len, dtype=jnp.int32)
    ang = pos.astype(F32)[:, None] * inv
    c, s = jnp.cos(ang), jnp.sin(ang)
    rest = HEAD_DIM - ROPE_DIM
    cos_h = jnp.concatenate([c, c, jnp.ones((t_len, rest), F32)], axis=1)
    sin_h = jnp.concatenate([-s, s, jnp.zeros((t_len, rest), F32)], axis=1)
    return jnp.tile(cos_h, (1, N_HEADS)), jnp.tile(sin_h, (1, N_HEADS))


def _inproj(x, t_len, q_pos0, gain, w_in, q_gain, k_gain, prefill, kv_prev=None):
    n, d = x.shape
    if t_len % 512 == 0:
        tm = 512
    else:
        tm = t_len * max(r for r in range(1, n // t_len + 1) if (n // t_len) % r == 0 and t_len * r <= 512)
    assert not prefill or (tm % MOBA_BLOCK == 0 and t_len // MOBA_BLOCK <= 128 - HEAD_DIM)
    cos_t, sin_t = _rope_tables(t_len, q_pos0)
    if tm > t_len:
        cos_t, sin_t = jnp.tile(cos_t, (tm // t_len, 1)), jnp.tile(sin_t, (tm // t_len, 1))
    per_seq = max(t_len // tm, 1)
    tok = lambda i: (i, 0)
    tab = lambda i: (i % per_seq, 0)
    wide = 2 * GROUP_W
    grp = jax.ShapeDtypeStruct((n, GROUP_W), F32)
    gspec = pl.BlockSpec((tm, GROUP_W), tok)
    wspec = pl.BlockSpec((tm, wide), tok)
    x4 = jax.ShapeDtypeStruct((n, 4 * GROUP_W), F32)
    x4spec = pl.BlockSpec((tm, 4 * GROUP_W), tok)
    n_prev = kv_prev[0].shape[0] if kv_prev is not None else 0
    prev_args, prev_specs = [], []
    if prefill:
        kv_map = lambda i: (0, i // per_seq, 0, i % per_seq)
        kv_t = jax.ShapeDtypeStruct((n_prev + 1, n // t_len, GROUP_W, t_len), F32)
        kv_tspec = pl.BlockSpec((n_prev + 1, 1, GROUP_W, tm), kv_map)
        if n_prev:
            prev_args = list(kv_prev)
            prev_specs = [pl.BlockSpec((n_prev, 1, GROUP_W, tm), kv_map)] * 2
        out_shape =[grp, kv_t, kv_t, jax.ShapeDtypeStruct((n, wide), F32), jax.ShapeDtypeStruct((n, wide), BF16),
                     jax.ShapeDtypeStruct((n, wide), BF16), grp, x4,
                     jax.ShapeDtypeStruct((n // MOBA_BLOCK, 1, wide), F32)]
        out_specs = [gspec, kv_tspec, kv_tspec, wspec, wspec, wspec, gspec, x4spec,
                     pl.BlockSpec((tm // MOBA_BLOCK, 1, wide), lambda i: (i, 0, 0))]
    else:
        out_shape = [grp, grp, grp, grp, grp, x4]
        out_specs = [gspec] * 5 + [x4spec]
    return pl.pallas_call(
        functools.partial(_inproj_kernel, prefill=prefill, tm=tm, per_seq=per_seq, n_prev=n_prev),
        grid=(n // tm,),
        in_specs=[pl.BlockSpec((tm, d), tok), _resident((1, d)), _resident(w_in.shape),
                  _resident((1, GROUP_W)), _resident((1, GROUP_W)),
                  pl.BlockSpec((tm, GROUP_W), tab), pl.BlockSpec((tm, GROUP_W), tab)] + prev_specs,
        out_specs=out_specs,
        out_shape=out_shape,
        compiler_params=_params(("parallel",)),
        name="inproj",
    )(x, gain.reshape(1, d), w_in, jnp.tile(q_gain, N_HEADS).reshape(1, GROUP_W),
      jnp.tile(k_gain, N_HEADS).reshape(1, GROUP_W), cos_t, sin_t, *prev_args)


def _poolconv_kernel(up_ref, cu_ref, pprev_ref, cprev_ref, pw_ref, ps_ref, cw_ref, cb_ref, lg_ref, lb_ref, cpw_ref,
                     ya_ref, yc_ref, pnew_ref, cnew_ref, pbuf, cbuf, zbuf, *, tt, q_pos0, rows):
    t = pl.program_id(1)

    @pl.when(t == 0)
    def _():
        pbuf[0:POOL_PAD, :] = pprev_ref[0]
        cbuf[0:CONV_PAD, :] = cprev_ref[0]
        cbuf[CONV_PAD + tt:CONV_PAD + tt + 8, :] = jnp.zeros((8, GROUP_W), F32)

    pbuf[POOL_PAD:POOL_PAD + tt, :] = up_ref[0]
    cbuf[CONV_PAD:CONV_PAD + tt, :] = cu_ref[0]

    for r0 in range(0, tt, rows):
        pos1 = (q_pos0 + t * tt + r0 + 1 + lax.broadcasted_iota(jnp.int32, (rows, 128), 0)).astype(F32)
        first = lax.broadcasted_iota(jnp.int32, (rows, 128), 1) < (GROUP_W // 4)
        pooled = []
        for half_i, (w_a, w_b) in enumerate(((POOL_WINDOWS[0], POOL_WINDOWS[1]), (POOL_WINDOWS[2], POOL_WINDOWS[3]))):
            lanes = slice(half_i * 128, (half_i + 1) * 128)
            u = pbuf[POOL_PAD + r0:POOL_PAD + r0 + rows, lanes]
            acc = u
            s_a = None
            for j in range(1, w_b):
                acc = acc + pbuf[POOL_PAD + r0 - j:POOL_PAD + r0 - j + rows, lanes]
                if j == w_a - 1:
                    s_a = acc
            avg = jnp.where(first, s_a / jnp.minimum(float(w_a), pos1), acc / jnp.minimum(float(w_b), pos1))
            pooled.append((avg - u).astype(BF16))
        pooled = jnp.concatenate(pooled, axis=1)
        ya_ref[0, r0:r0 + rows, :] = _dot(pooled, pw_ref[...]) * ps_ref[...]

        ext = rows + 8
        for s in range(8):
            z = None
            for a in range(-(-(CONV_W + CONV_LEAD) // 8)):
                j = 8 * a + s - CONV_LEAD
                if 0 <= j < CONV_W:
                    term = cbuf[r0 + 8 * a:r0 + 8 * a + ext, :] * cw_ref[j:j + 1, :]
                    z = term if z is None else z + term
            zbuf[s] = z
        acc = zbuf[0, 0:rows, :]
        for s in range(1, 8):
            acc = acc + zbuf[s, s:s + rows, :]
        y = acc + cb_ref[...]
        mu = jnp.mean(y, axis=-1, keepdims=True)
        var = jnp.mean(jnp.square(y - mu), axis=-1, keepdims=True)
        y = (y - mu) * lax.rsqrt(var + NORM_EPS) * lg_ref[...] + lb_ref[...]
        y = y * _sigmoid(y)
        yc_ref[0, r0:r0 + rows, :] = _dot(y.astype(BF16), cpw_ref[...])

    ptail = pbuf[tt:tt + POOL_PAD, :]
    ctail = cbuf[tt:tt + CONV_PAD, :]
    pnew_ref[0] = ptail
    cnew_ref[0] = ctail
    pbuf[0:POOL_PAD, :] = ptail
    cbuf[0:CONV_PAD, :] = ctail


def _poolconv(up, cu, pool_prev, conv_prev, q_pos0, pool_w, pool_scale, conv_w, conv_b, ln_g, ln_b, conv_pw):
    b, t_len, _ = up.shape
    tt = 512 if t_len % 512 == 0 else t_len
    rows = 64 if tt % 64 == 0 else tt
    pprev = jnp.pad(pool_prev, ((0, 0), (POOL_PAD - pool_prev.shape[1], 0), (0, 0)))
    cprev = jnp.pad(conv_prev, ((0, 0), (CONV_PAD - conv_prev.shape[1], 0), (0, 0)))
    pw_bd = jax.scipy.linalg.block_diag(*[pool_w[g] for g in range(pool_w.shape[0])]).astype(BF16)
    cw_pad = jnp.pad(conv_w, ((0, CONV_PAD - CONV_W), (0, 0)))
    seq = lambda i, j: (i, j, 0)
    per_b = lambda i, j: (i, 0, 0)
    const = lambda i, j: (0, 0)
    row = pl.BlockSpec((1, GROUP_W), const)
    ya, yc, pnew, cnew = pl.pallas_call(
        functools.partial(_poolconv_kernel, tt=tt, q_pos0=q_pos0, rows=rows),
        grid=(b, t_len // tt),
        in_specs=[pl.BlockSpec((1, tt, GROUP_W), seq), pl.BlockSpec((1, tt, GROUP_W), seq),
                  pl.BlockSpec((1, POOL_PAD, GROUP_W), per_b), pl.BlockSpec((1, CONV_PAD, GROUP_W), per_b),
                  pl.BlockSpec((GROUP_W, GROUP_W), const), row,
                  pl.BlockSpec((CONV_PAD, GROUP_W), const), row, row, row,
                  pl.BlockSpec((GROUP_W, GROUP_W), const)],
        out_specs=[pl.BlockSpec((1, tt, GROUP_W), seq), pl.BlockSpec((1, tt, GROUP_W), seq),
                   pl.BlockSpec((1, POOL_PAD, GROUP_W), per_b), pl.BlockSpec((1, CONV_PAD, GROUP_W), per_b)],
        out_shape=[jax.ShapeDtypeStruct((b, t_len, GROUP_W), F32), jax.ShapeDtypeStruct((b, t_len, GROUP_W), F32),
                   jax.ShapeDtypeStruct((b, POOL_PAD, GROUP_W), F32), jax.ShapeDtypeStruct((b, CONV_PAD, GROUP_W), F32)],
        scratch_shapes=[pltpu.VMEM((POOL_PAD + tt, GROUP_W), F32), pltpu.VMEM((CONV_PAD + tt + 8, GROUP_W), F32),
                        pltpu.VMEM((8, rows + 8, GROUP_W), F32)],
        compiler_params=_params(("parallel", "arbitrary")),
        name="poolconv",
    )(up, cu, pprev, cprev, pw_bd, pool_scale.reshape(1, GROUP_W), cw_pad, conv_b.reshape(1, GROUP_W),
      ln_g.reshape(1, GROUP_W), ln_b.reshape(1, GROUP_W), conv_pw.astype(BF16))
    return ya, yc, pnew[:, POOL_PAD - pool_prev.shape[1]:], cnew[:, CONV_PAD - conv_prev.shape[1]:]


def _stack_heads(q):
    head = _lane_head(q.shape)
    return jnp.concatenate([jnp.where(head == h, q, 0.0) for h in range(N_HEADS)], axis=0)


def _select_blocks(gate, n_past):
    lane = lax.broadcasted_iota(jnp.int32, gate.shape, 1)
    past = lane < n_past
    g = jnp.where(past, gate, NEG_BIG)
    sel = jnp.zeros(gate.shape, jnp.bool_)
    for _ in range(MOBA_TOPK):
        mx = jnp.max(g, axis=1, keepdims=True)
        idx = jnp.min(jnp.where(g == mx, lane, 1 << 20), axis=1, keepdims=True)
        pick = lane == idx
        sel = jnp.logical_or(sel, pick)
        g = jnp.where(pick, -jnp.inf, g)
    return jnp.logical_and(sel, past)


def _select_block_rows(gate_t, n_past, nblk):
    n = lax.broadcasted_iota(jnp.int32, gate_t.shape, 0)
    past = n < n_past
    g = jnp.where(past, gate_t, jnp.where(n < nblk, NEG_BIG, -jnp.inf))
    sel = jnp.zeros(gate_t.shape, jnp.bool_)
    for _ in range(MOBA_TOPK):
        mx = jnp.max(g, axis=0, keepdims=True)
        idx = jnp.min(jnp.where(g == mx, n, 1 << 20), axis=0, keepdims=True)
        pick = n == idx
        sel = jnp.logical_or(sel, pick)
        g = jnp.where(pick, -jnp.inf, g)
    return jnp.logical_and(sel, past)


def _moba_prefill_kernel(qp_ref, ka_ref, va_ref, km_ref, o_ref,
                         kmr_ref, qa_ref, s_ref, mx_ref, m_ref, acc_ref, *, nblk):
    i = pl.program_id(1)
    blk = MOBA_BLOCK
    kb = kmr_ref.shape[1]
    heads = [slice(h * 128, (h + 1) * 128) for h in range(N_HEADS)]

    @pl.when(i == 0)
    def _():
        kmr_ref[...] = jnp.zeros_like(kmr_ref)
        for h in range(N_HEADS):
            for n in range(nblk):
                kmr_ref[h, n:n + 1, :] = km_ref[0, n][:, heads[h]]

    n_row = lax.broadcasted_iota(jnp.int32, (kb, blk), 0)
    for h in range(N_HEADS):
        qh = qp_ref[0, :, heads[h]]
        gate_t = _dot_nt(kmr_ref[h], qh, precision=HI)
        sel = _select_block_rows(gate_t, i, nblk)
        bias_t = jnp.where(jnp.logical_or(sel, n_row == i), 0.0, NEG_BIG)
        bias_t = jnp.concatenate([jnp.zeros((HEAD_DIM, blk), F32), bias_t,
                                  jnp.zeros((128 - HEAD_DIM - kb, blk), F32)], axis=0)
        qa_ref[h] = (qh * (ATTN_SCALE * LOG2_E) + bias_t.T).astype(BF16)
        mx_ref[h] = jnp.full((blk, 128), NEG_BIG, F32)
        acc_ref[h] = jnp.zeros((blk, 128), F32)

    row = lax.broadcasted_iota(jnp.int32, (blk, blk), 0)
    col = lax.broadcasted_iota(jnp.int32, (blk, blk), 1)
    n_pairs = (i + 2) // 2

    def scores(jj, carry):
        for u in range(2):
            j = 2 * jj + u
            j0 = pl.multiple_of(j * blk, blk)
            allow = col + (j - i) * blk <= row
            for h in range(N_HEADS):
                s = jnp.where(allow, _dot_nt(qa_ref[h], ka_ref[0, pl.ds(j0, blk), heads[h]]), NEG_BIG)
                s_ref[h, j] = s
                mx_ref[h] = jnp.maximum(mx_ref[h], jnp.maximum(s[:, :128], s[:, 128:]))
        return carry

    lax.fori_loop(0, n_pairs, scores, 0)
    for h in range(N_HEADS):
        m_ref[h] = jnp.broadcast_to(jnp.max(mx_ref[h], axis=1, keepdims=True), (blk, 128))

    def weighted(jj, carry):
        for u in range(2):
            j = 2 * jj + u
            j0 = pl.multiple_of(j * blk, blk)
            for h in range(N_HEADS):
                s = s_ref[h, j]
                m = m_ref[h]
                p = jnp.concatenate([jnp.exp2(s[:, :128] - m), jnp.exp2(s[:, 128:] - m)], axis=1).astype(BF16)
                acc_ref[h] += _dot(p, va_ref[0, pl.ds(j0, blk), heads[h]])
        return carry

    lax.fori_loop(0, n_pairs, weighted, 0)
    lane = lax.broadcasted_iota(jnp.int32, (blk, 128), 1)
    for h in range(N_HEADS):
        acc = acc_ref[h]
        o_ref[0, :, heads[h]] = jnp.where(lane < HEAD_DIM, acc / acc[:, HEAD_DIM:HEAD_DIM + 1], 0.0)


def _moba_prefill(qp, ka, va, kmean):
    b, t_len, wide = qp.shape
    nblk = t_len // MOBA_BLOCK
    assert nblk % 2 == 0 and nblk <= 128 - HEAD_DIM
    kb = -(-nblk // 8) * 8
    blk = MOBA_BLOCK
    per_b = lambda i, j: (i, 0, 0)
    return pl.pallas_call(
        functools.partial(_moba_prefill_kernel, nblk=nblk),
        grid=(b, nblk),
        in_specs=[pl.BlockSpec((1, blk, wide), lambda i, j: (i, j, 0)),
                  pl.BlockSpec((1, t_len, wide), per_b, pipeline_mode=pl.Buffered(1)),
                  pl.BlockSpec((1, t_len, wide), per_b, pipeline_mode=pl.Buffered(1)),
                  pl.BlockSpec((1, nblk, 1, wide), lambda i, j: (i, 0, 0, 0))],
        out_specs=pl.BlockSpec((1, blk, wide), lambda i, j: (i, j, 0)),
        out_shape=jax.ShapeDtypeStruct((b, t_len, wide), F32),
        scratch_shapes=[pltpu.VMEM((N_HEADS, kb, 128), F32), pltpu.VMEM((N_HEADS, blk, 128), BF16),
                        pltpu.VMEM((N_HEADS, nblk, blk, blk), F32), pltpu.VMEM((N_HEADS, blk, 128), F32),
                        pltpu.VMEM((N_HEADS, blk, 128), F32), pltpu.VMEM((N_HEADS, blk, 128), F32)],
        compiler_params=_params(("parallel", "arbitrary")),
        name="moba_prefill",
    )(qp, ka, va, kmean)


def _moba_decode_kernel(pt_ref, q_ref, kn_ref, vn_ref, ck_ref, cv_ref, o_ref,
                        buf, sem, s_ref, p_ref, own_ref, *, layer, n_pages, tq):
    b = pl.program_id(0)
    ch = PAGES_PER_CHUNK
    n_chunks = n_pages // ch
    page = buf.shape[3]
    pages_per_blk = MOBA_BLOCK // page
    n_past = n_pages // pages_per_blk
    rows = N_HEADS * tq

    def page_copy(cache_ref, c, pg, slot, seq=b):
        return pltpu.make_async_copy(cache_ref.at[layer, pt_ref[seq, c * ch + pg]], buf.at[slot, pg], sem.at[slot])

    def start_chunk(cache_ref, c, slot, seq=b):
        for pg in range(ch):
            page_copy(cache_ref, c, pg, slot, seq).start()

    def wait_chunk(cache_ref, c, slot):
        for pg in range(ch):
            page_copy(cache_ref, c, pg, slot).wait()

    q4 = _stack_heads(q_ref[0]) * ATTN_SCALE
    q_hi = q4.astype(BF16)
    qq = jnp.concatenate([q_hi, (q4 - q_hi.astype(F32)).astype(BF16)], axis=0)

    def both(s2):
        return s2[:rows] + s2[rows:]

    @pl.when(b == 0)
    def _():
        start_chunk(ck_ref, 0, 0)

    def k_chunk(c, carry):
        slot = c % 2

        @pl.when(c + 1 < n_chunks)
        def _():
            start_chunk(ck_ref, c + 1, 1 - slot)

        @pl.when(c + 1 == n_chunks)
        def _():
            start_chunk(cv_ref, 0, 1 - slot)

        wait_chunk(ck_ref, c, slot)
        for pg in range(ch):
            s_ref[c * ch + pg] = both(_dot(qq, buf[slot, pg].astype(BF16)))
        return carry

    lax.fori_loop(0, n_chunks, k_chunk, 0)

    lane = lax.broadcasted_iota(jnp.int32, (rows, 128), 1)
    gate = jnp.zeros((rows, 128), F32)
    for n in range(n_past):
        tot = s_ref[n * pages_per_blk]
        for pg in range(1, pages_per_blk):
            tot = tot + s_ref[n * pages_per_blk + pg]
        gate = jnp.where(lane == n, jnp.sum(tot, axis=1, keepdims=True), gate)
    gate = gate * (1.0 / (MOBA_BLOCK * ATTN_SCALE))
    sel = _select_blocks(gate, n_past).astype(F32)

    own_ref[...] = jnp.zeros_like(own_ref)
    own_ref[0, 0:tq, :] = kn_ref[0]
    own_ref[1, 0:tq, :] = vn_ref[0]
    s_own = both(_dot_nt(qq, own_ref[0].astype(BF16)))
    s_own = jnp.where(lane <= lax.broadcasted_iota(jnp.int32, (rows, 128), 0) % tq, s_own, NEG_BIG)

    mx = s_own
    for n in range(n_past):
        ok = sel[:, n:n + 1] > 0.5
        for pg in range(n * pages_per_blk, (n + 1) * pages_per_blk):
            s_pg = jnp.where(ok, s_ref[pg], NEG_BIG)
            s_ref[pg] = s_pg
            mx = jnp.maximum(mx, s_pg)
    m = jnp.max(mx, axis=1, keepdims=True)
    p_own = jnp.exp(s_own - m)
    lsum = p_own
    for pg in range(n_pages):
        p = jnp.exp(s_ref[pg] - m)
        lsum = lsum + p
        p_ref[pg] = p.astype(BF16)
    l = jnp.sum(lsum, axis=1, keepdims=True)

    def v_chunk(c, acc):
        slot = (n_chunks + c) % 2

        @pl.when(c + 1 < n_chunks)
        def _():
            start_chunk(cv_ref, c + 1, 1 - slot)

        @pl.when(jnp.logical_and(c + 1 == n_chunks, b + 1 < pl.num_programs(0)))
        def _():
            start_chunk(ck_ref, 0, 1 - slot, seq=b + 1)

        wait_chunk(cv_ref, c, slot)
        for pg in range(ch):
            acc = acc + _dot_nt(p_ref[c * ch + pg], buf[slot, pg].astype(BF16))
        return acc

    acc = lax.fori_loop(0, n_chunks, v_chunk, _dot(p_own.astype(BF16), own_ref[1].astype(BF16)))
    acc = acc / l
    head = _lane_head((tq, GROUP_W))
    out = jnp.zeros((tq, GROUP_W), F32)
    for h in range(N_HEADS):
        out = out + jnp.where(head == h, acc[h * tq:(h + 1) * tq, :], 0.0)
    o_ref[0] = out


def _moba_decode(q, k_new, v_new, cache_k, cache_v, page_table, layer):
    b, tq, _ = q.shape
    n_pages = page_table.shape[1]
    page = cache_k.shape[3]
    assert page == 128 and MOBA_BLOCK % page == 0 and n_pages % (MOBA_BLOCK // page) == 0
    assert n_pages % PAGES_PER_CHUNK == 0 and tq <= 128 and n_pages * page // MOBA_BLOCK <= 128
    rows = N_HEADS * tq
    per_b = lambda i, pt: (i, 0, 0)
    grid_spec = pltpu.PrefetchScalarGridSpec(
        num_scalar_prefetch=1,
        grid=(b,),
        in_specs=[pl.BlockSpec((1, tq, GROUP_W), per_b), pl.BlockSpec((1, tq, GROUP_W), per_b),
                  pl.BlockSpec((1, tq, GROUP_W), per_b),
                  pl.BlockSpec(memory_space=pl.ANY), pl.BlockSpec(memory_space=pl.ANY)],
        out_specs=pl.BlockSpec((1, tq, GROUP_W), per_b),
        scratch_shapes=[pltpu.VMEM((2, PAGES_PER_CHUNK, GROUP_W, page), F32),
                        pltpu.SemaphoreType.DMA((2,)),
                        pltpu.VMEM((n_pages, rows, page), F32), pltpu.VMEM((n_pages, rows, page), BF16),
                        pltpu.VMEM((2, 128, GROUP_W), F32)],
    )
    return pl.pallas_call(
        functools.partial(_moba_decode_kernel, layer=layer, n_pages=n_pages, tq=tq),
        grid_spec=grid_spec,
        out_shape=jax.ShapeDtypeStruct((b, tq, GROUP_W), F32),
        compiler_params=_params(("arbitrary",)),
        name="moba_decode",
    )(page_table, q, k_new, v_new, cache_k, cache_v)


def _hgrn_kernel(x_ref, st0_ref, lb_ref, ng_ref, y_ref, stout_ref, st_ref, qst_ref, upd_ref, dec_ref,
                 *, tt, chunk, sub):
    t = pl.program_id(1)
    n_sub = chunk // sub

    bd = _head_ones()
    spread = (lax.broadcasted_iota(jnp.int32, (GROUP_W, HEAD_DIM), 0) % HEAD_DIM
              == lax.broadcasted_iota(jnp.int32, (GROUP_W, HEAD_DIM), 1)).astype(F32)

    @pl.when(t == 0)
    def _():
        st_ref[...] = jnp.where(bd, _dot_nt(spread, st0_ref[0], precision=HI), 0.0)

    lb = lb_ref[...]
    head = _lane_head((1, GROUP_W))
    hmask = [head == h for h in range(N_HEADS)]
    ones_b = bd.astype(BF16)
    ones_f = bd.astype(F32)
    tri = (lax.broadcasted_iota(jnp.int32, (chunk, chunk), 1)
           <= lax.broadcasted_iota(jnp.int32, (chunk, chunk), 0)).astype(F32)
    t_ge = [lax.broadcasted_iota(jnp.int32, (sub, GROUP_W), 0) >= s for s in range(sub)]

    n_chunks = tt // chunk
    if n_sub > 1:
        n_used = sub * (n_sub * (n_sub - 1) // 2)
        n_col = -(-n_used // 128) * 128
        col = lax.broadcasted_iota(jnp.int32, (N_HEADS * chunk, n_col), 1)
        row_sub = (lax.broadcasted_iota(jnp.int32, (N_HEADS * chunk, n_col), 0) % chunk) // sub
        keep = jnp.zeros((N_HEADS * chunk, n_col), jnp.bool_)
        c0 = 0
        for i in range(1, n_sub):
            keep = jnp.logical_or(keep, jnp.logical_and(row_sub == i, jnp.logical_and(col >= c0, col < c0 + i * sub)))
            c0 += i * sub

    def local_part(ci):
        r0 = ci * chunk if isinstance(ci, int) else pl.multiple_of(ci * chunk, chunk)
        hq = x_ref[0, pl.ds(r0, chunk), 0 * GROUP_W:1 * GROUP_W]
        hf = x_ref[0, pl.ds(r0, chunk), 1 * GROUP_W:2 * GROUP_W]
        hi = x_ref[0, pl.ds(r0, chunk), 2 * GROUP_W:3 * GROUP_W]
        f = lb + (1.0 - lb) * _sigmoid(hf)
        kin = 1.0 - f
        bc = jnp.dot(tri, jnp.log2(f), precision=HI, preferred_element_type=F32)
        b_last = bc[chunk - 1:chunk, :]
        qst_ref[pl.ds(r0, chunk), :] = hq * jnp.exp2(bc)
        k_state = (kin * jnp.exp2(b_last - bc)).astype(BF16)
        upd_ref[ci] = jnp.where(bd, _dot(hi.T.astype(BF16), k_state), 0.0)
        dec_ref[ci] = jnp.exp2(b_last)

        o = None
        if n_sub > 1:
            bq = jnp.concatenate(
                [jnp.zeros((sub, GROUP_W), F32)]
                + [jnp.broadcast_to(bc[i * sub - 1:i * sub, :], (sub, GROUP_W)) for i in range(1, n_sub)], axis=0)
            q_off = hq * jnp.exp2(bc - bq)
            q4 = jnp.concatenate([jnp.where(hmask[h], q_off, 0.0) for h in range(N_HEADS)], axis=0).astype(BF16)
            pad = [jnp.zeros((n_col - n_used, GROUP_W), F32)] if n_col > n_used else []
            k_st = jnp.concatenate(
                [kin[0:i * sub, :] * jnp.exp2(bc[i * sub - 1:i * sub, :] - bc[0:i * sub, :])
                 for i in range(1, n_sub)] + pad, axis=0).astype(BF16)
            v_st = jnp.concatenate([hi[0:i * sub, :] for i in range(1, n_sub)] + pad, axis=0).astype(BF16)
            att = jnp.where(keep, _dot_nt(q4, k_st), 0.0).astype(BF16)
            res = _dot(att, v_st)
            o = jnp.where(hmask[0], res[0:chunk], 0.0)
            for h in range(1, N_HEADS):
                o = o + jnp.where(hmask[h], res[h * chunk:(h + 1) * chunk], 0.0)

        d_rows = []
        for i in range(n_sub):
            rs = slice(i * sub, (i + 1) * sub)
            q_i, bc_i, k_i = hq[rs], bc[rs], kin[rs]
            for s in range(sub):
                e = jnp.exp2(jnp.where(t_ge[s], bc_i - bc_i[s:s + 1, :], NEG_BIG))
                d_rows.append(q_i * e * k_i[s:s + 1, :])
        a = _dot(jnp.concatenate(d_rows, axis=0).astype(BF16), ones_b)
        o_diag = []
        for i in range(n_sub):
            v_i = hi[i * sub:(i + 1) * sub]
            acc = a[i * sub * sub:i * sub * sub + sub, :] * v_i[0:1, :]
            for s in range(1, sub):
                base = (i * sub + s) * sub
                acc = acc + a[base:base + sub, :] * v_i[s:s + 1, :]
            o_diag.append(acc)
        o_diag = jnp.concatenate(o_diag, axis=0) if n_sub > 1 else o_diag[0]
        y_ref[0, pl.ds(r0, chunk), :] = o_diag if o is None else o + o_diag

    def carried_part(ci, carry):
        r0 = pl.multiple_of(ci * chunk, chunk)
        st = st_ref[...]
        y_ref[0, pl.ds(r0, chunk), :] += _dot_nt(qst_ref[pl.ds(r0, chunk), :].astype(BF16), st.astype(BF16))
        st_ref[...] = st * dec_ref[ci] + upd_ref[ci]
        return carry

    if n_chunks % 2 == 0:
        def pair(cp, carry):
            local_part(2 * cp)
            local_part(2 * cp + 1)
            return carry
        lax.fori_loop(0, n_chunks // 2, pair, 0)
    else:
        for ci in range(n_chunks):
            local_part(ci)
    lax.fori_loop(0, n_chunks, carried_part, 0)

    o = y_ref[0]
    hg = x_ref[0, :, 3 * GROUP_W:4 * GROUP_W]
    ms = jnp.dot(o * o, ones_f, precision=HI, preferred_element_type=F32) * (1.0 / HEAD_DIM)
    y_ref[0] = o * lax.rsqrt(ms + NORM_EPS) * ng_ref[...] * (hg * _sigmoid(hg))

    @pl.when(t == pl.num_programs(1) - 1)
    def _():
        stout_ref[0] = jnp.dot(st_ref[...].T, spread, precision=HI, preferred_element_type=F32)


def _hgrn(x4, state, lb, norm_g):
    b, t_len, _ = x4.shape
    chunk = HGRN_CHUNK if t_len % HGRN_CHUNK == 0 else t_len
    sub = HGRN_SUB if chunk % HGRN_SUB == 0 else chunk
    tt = 512 if t_len % 512 == 0 else (chunk if t_len == chunk else t_len)
    st_rows = pl.BlockSpec((1, GROUP_W, HEAD_DIM), lambda i, j: (i, 0, 0))
    y, st = pl.pallas_call(
        functools.partial(_hgrn_kernel, tt=tt, chunk=chunk, sub=sub),
        grid=(b, t_len // tt),
        in_specs=[pl.BlockSpec((1, tt, 4 * GROUP_W), lambda i, j: (i, j, 0)), st_rows,
                  pl.BlockSpec((1, GROUP_W), lambda i, j: (0, 0)),
                  pl.BlockSpec((1, GROUP_W), lambda i, j: (0, 0))],
        out_specs=[pl.BlockSpec((1, tt, GROUP_W), lambda i, j: (i, j, 0)), st_rows],
        out_shape=[jax.ShapeDtypeStruct((b, t_len, GROUP_W), F32),
                   jax.ShapeDtypeStruct((b, GROUP_W, HEAD_DIM), F32)],
        scratch_shapes=[pltpu.VMEM((GROUP_W, GROUP_W), F32), pltpu.VMEM((tt, GROUP_W), F32),
                        pltpu.VMEM((tt // chunk, GROUP_W, GROUP_W), F32), pltpu.VMEM((tt // chunk, 1, GROUP_W), F32)],
        compiler_params=_params(("parallel", "arbitrary")),
        name="hgrn",
    )(x4, state.reshape(b, GROUP_W, HEAD_DIM), lb.reshape(1, GROUP_W), jnp.tile(norm_g, N_HEADS).reshape(1, GROUP_W))
    return y, st.reshape(state.shape)


def _trunk(x, q_pos0, pool_st, conv_st, hgrn_st, cache_k, cache_v, page_table, w):
    b, t_len, d = x.shape
    n = b * t_len
    depth = w["ln_ffn1"].shape[0]
    lbs = jax.nn.softmax(w["hgrn_lower_bounds"].astype(F32), axis=0)
    lbs = jnp.cumsum(lbs, axis=0) - lbs[0]
    prefill = cache_k is None
    xf = x.reshape(n, d)
    ks, vs, pools, convs, hgrns = [], [], [], [], []
    kv_stack = None
    for l in range(depth):
        xf = _ffn(xf, w["ln_ffn1"][l], w["ffn1_w_gate"][l], w["ffn1_w_up"][l], w["ffn1_w_down"][l])
        outs = _inproj(xf, t_len, q_pos0, w["ln_mix"][l], w["w_in"][l], w["q_norm"][l], w["k_norm"][l], prefill,
                       kv_prev=kv_stack)
        seq = lambda a: a.reshape(b, t_len, a.shape[-1])
        wo = w["w_out"][l]
        wo_parts = [wo[g * GROUP_W:(g + 1) * GROUP_W] for g in range(4)]
        if prefill:
            up, kt, vt, qp, ka, va, cu, x4, kmean = outs
            kv_stack = (kt, vt)
            yb =_moba_prefill(seq(qp), seq(ka), seq(va), kmean.reshape(b, t_len // MOBA_BLOCK, 1, 2 * GROUP_W))
            wo_parts[1] = jnp.pad(wo_parts[1].reshape(N_HEADS, HEAD_DIM, d),
                                  ((0, 0), (0, 128 - HEAD_DIM), (0, 0))).reshape(N_HEADS * 128, d)
        else:
            up, q, k, v, cu, x4 = outs
            yb = _moba_decode(seq(q), seq(k), seq(v), cache_k, cache_v, page_table, l)
            ks.append(k.reshape(b, t_len, N_HEADS, HEAD_DIM))
            vs.append(v.reshape(b, t_len, N_HEADS, HEAD_DIM))
        ya, yc, pool_new, conv_new = _poolconv(
            seq(up), seq(cu), pool_st[l], conv_st[l], q_pos0, w["pool_w"][l], w["pool_scale"][l],
            w["conv_w"][l], w["conv_b"][l], w["conv_ln_g"][l], w["conv_ln_b"][l], w["conv_pw"][l])
        yd, s_new = _hgrn(seq(x4), hgrn_st[l], lbs[l], w["hgrn_norm"][l])
        flat = lambda a: a.reshape(n, a.shape[-1])
        xf = _ffn(xf, w["ln_ffn2"][l], w["ffn2_w_gate"][l], w["ffn2_w_up"][l], w["ffn2_w_down"][l],
                  mix=tuple(zip((flat(ya), flat(yb), flat(yc), flat(yd)), wo_parts)))
        pools.append(pool_new)
        convs.append(conv_new)
        hgrns.append(s_new)
    if prefill:
        k_all, v_all = (jnp.transpose(a.reshape(depth, b, N_HEADS, HEAD_DIM, t_len), (0, 1, 4, 2, 3)) for a in kv_stack)
    else:
        k_all, v_all = jnp.stack(ks), jnp.stack(vs)
    return (xf.reshape(b, t_len, d), k_all, v_all, jnp.stack(pools), jnp.stack(convs), jnp.stack(hgrns))


def kernel(x_prompt, x_sample, cache_k, cache_v, page_table, state_pool, state_conv, state_hgrn, ln_ffn1, ffn1_w_gate, ffn1_w_up, ffn1_w_down, ln_mix, w_in, w_out, pool_w, pool_scale, q_norm, k_norm, conv_w, conv_b, conv_ln_g, conv_ln_b, conv_pw, hgrn_lower_bounds, hgrn_norm, ln_ffn2, ffn2_w_gate, ffn2_w_up, ffn2_w_down):
    w = dict(ln_ffn1=ln_ffn1, ln_mix=ln_mix, ln_ffn2=ln_ffn2, pool_w=pool_w, pool_scale=pool_scale,
             q_norm=q_norm, k_norm=k_norm, conv_w=conv_w, conv_b=conv_b, conv_ln_g=conv_ln_g,
             conv_ln_b=conv_ln_b, conv_pw=conv_pw, hgrn_lower_bounds=hgrn_lower_bounds, hgrn_norm=hgrn_norm)
    for name, arr in (("ffn1_w_gate", ffn1_w_gate), ("ffn1_w_up", ffn1_w_up), ("ffn1_w_down", ffn1_w_down),
                      ("w_in", w_in), ("w_out", w_out), ("ffn2_w_gate", ffn2_w_gate), ("ffn2_w_up", ffn2_w_up),
                      ("ffn2_w_down", ffn2_w_down)):
        w[name] = arr.astype(BF16)
    depth = ln_ffn1.shape[0]
    bp = x_prompt.shape[0]
    dt = x_prompt.dtype
    zero_pool = jnp.zeros((depth, bp) + state_pool.shape[2:], dt)
    zero_conv = jnp.zeros((depth, bp) + state_conv.shape[2:], dt)
    zero_hgrn = jnp.zeros((depth, bp) + state_hgrn.shape[2:], dt)
    y_p, k_p, v_p, pool_p, conv_p, hgrn_p = _trunk(
        x_prompt, 0, zero_pool, zero_conv, zero_hgrn, None, None, None, w)
    past_len = page_table.shape[1] * cache_k.shape[2]
    to_pages = lambda c: jnp.transpose(c, (0, 1, 3, 4, 2)).reshape(c.shape[:2] + (GROUP_W, c.shape[2]))
    ck, cv = to_pages(cache_k), to_pages(cache_v)
    y_s, k_s, v_s, pool_s, conv_s, hgrn_s = _trunk(
        x_sample, past_len, state_pool, state_conv, state_hgrn, ck, cv, page_table, w)
    return (y_p, y_s, k_p, v_p, k_s, v_s, pool_p, pool_s, conv_p, conv_s, hgrn_p, hgrn_s)
```
